```python
import math
import jax, jax.numpy as jnp
from jax import lax
import numpy as np

D_MODEL = 1024
BATCH = 32
SEQ = 256
DEPTH = 2
DEC_BATCH = 2
DEC_SEQ = 1024
PAST_LEN = 256

GRID_W = 64
CONV_W = 512
CONV_K = 31
CONV_PAD = (CONV_K - 1) // 2
SSM_W = 512
SSM_GC = 16
SSM_G = SSM_W // SSM_GC
SSM_P = 64
N_DIR = 2
N_HEADS = 8
N_KV_HEADS = 2
HEAD_DIM = 64
GQ = N_HEADS // N_KV_HEADS
ATTN_W = N_HEADS * HEAD_DIM
KV_W = N_KV_HEADS * HEAD_DIM
ROPE_PAIRS = HEAD_DIM // 4
ROPE_BASE = 10000.0
Q_BLOCK = 128
N_BRANCH = 3
EPS = 1e-6
IN_SIZES = (2 * CONV_W, CONV_W, SSM_W, SSM_W, ATTN_W, KV_W, KV_W, ATTN_W, N_BRANCH * D_MODEL)
IN_COLS = sum(IN_SIZES)

kernel_name = 'hybrid_conv_s5_gqa_prefix_dit_step'

F32 = jnp.float32


def _rmsnorm(x, g):
    xf = x.astype(F32)
    y = xf * lax.rsqrt(jnp.mean(xf * xf, axis=-1, keepdims=True) + EPS)
    return (y * g.astype(F32)).astype(x.dtype)


def _layernorm(x, g, b):
    xf = x.astype(F32)
    mu = jnp.mean(xf, axis=-1, keepdims=True)
    xc = xf - mu
    y = xc * lax.rsqrt(jnp.mean(xc * xc, axis=-1, keepdims=True) + EPS)
    return (y * g.astype(F32) + b.astype(F32)).astype(x.dtype)


def _adaln(cond, w, b):
    m = jax.nn.silu(cond) @ w + b
    return jnp.split(m[:, None, :], 3, axis=-1)


def _grid_angles(n_tok):
    rows = n_tok // GRID_W
    row = jnp.repeat(jnp.arange(rows, dtype=F32), GRID_W)
    col = jnp.tile(jnp.arange(GRID_W, dtype=F32), rows)
    inv = ROPE_BASE ** (-jnp.arange(ROPE_PAIRS, dtype=F32) / ROPE_PAIRS)
    return row[:, None] * inv[None, :], col[:, None] * inv[None, :]


def _rotate(xp, ang):
    x1, x2 = xp[..., :ROPE_PAIRS], xp[..., ROPE_PAIRS:]
    cos = jnp.cos(ang)[None, :, None, :]
    sin = jnp.sin(ang)[None, :, None, :]
    return jnp.concatenate([x1 * cos - x2 * sin, x1 * sin + x2 * cos], axis=-1)


def _axial_rope(x, angles):
    ang_row, ang_col = angles
    xf = x.astype(F32)
    half = HEAD_DIM // 2
    out = jnp.concatenate([_rotate(xf[..., :half], ang_row), _rotate(xf[..., half:], ang_col)], axis=-1)
    return out.astype(x.dtype)


def _blocked_attention(q, k, v):
    bsz, lq = q.shape[:2]
    nb = lq // Q_BLOCK
    qb = q.reshape(bsz, nb, Q_BLOCK, N_KV_HEADS, GQ, HEAD_DIM).transpose(1, 0, 2, 3, 4, 5)
    scale = HEAD_DIM ** -0.5

    def one_block(qblk):
        s = jnp.einsum('bqkgd,bskd->bkgqs', qblk, k).astype(F32) * scale
        p = jax.nn.softmax(s, axis=-1).astype(v.dtype)
        return jnp.einsum('bkgqs,bskd->bqkgd', p, v)

    o = lax.map(one_block, qb)
    return o.transpose(1, 0, 2, 3, 4, 5).reshape(bsz, lq, ATTN_W)


def _conformer_conv(u, dw_w, dw_b, ln_g, ln_b):
    a, g = jnp.split(u, 2, axis=-1)
    y = a * jax.nn.sigmoid(g)
    y = lax.conv_general_dilated(
        y, dw_w[:, None, :].astype(y.dtype), window_strides=(1,), padding=[(CONV_PAD, CONV_PAD)],
        dimension_numbers=('NWC', 'WIO', 'NWC'), feature_group_count=CONV_W) + dw_b.astype(y.dtype)
    return jax.nn.silu(_layernorm(y, ln_g, ln_b))


def _ssm_combine(e1, e2):
    a1, b1 = e1
    a2, b2 = e2
    return a1 * a2, a2 * b1 + b2


def _s5_bidirectional(u, a_re, a_im, log_dt, b_re, b_im, c_re, c_im, d, init_re, init_im):
    bsz, n = u.shape[:2]
    uf = u.astype(F32).reshape(bsz, n, SSM_G, SSM_GC)
    uc = uf.astype(jnp.complex64)
    y = d.astype(F32).reshape(SSM_G, SSM_GC) * uf
    want_final = init_re is None
    fin_re, fin_im = [], []
    for dirn in range(N_DIR):
        rev = dirn == 1
        lam = lax.complex(a_re[dirn].astype(F32), a_im[dirn].astype(F32))
        dt = jnp.exp(log_dt[dirn].astype(F32))[:, None]
        lam_bar = jnp.exp(lam * dt)
        b_bar = ((lam_bar - 1.0) / lam)[..., None] * lax.complex(b_re[dirn].astype(F32), b_im[dirn].astype(F32))
        bu = jnp.einsum('blgc,gpc->blgp', uc, b_bar)
        if not want_final:
            h0 = lax.complex(init_re[:, dirn].astype(F32), init_im[:, dirn].astype(F32))
            edge = n - 1 if rev else 0
            bu = bu.at[:, edge].add(lam_bar * h0)
        a = jnp.broadcast_to(lam_bar, bu.shape)
        _, hs = lax.associative_scan(_ssm_combine, (a, bu), axis=1, reverse=rev)
        if want_final:
            fin = hs[:, 0] if rev else hs[:, -1]
            fin_re.append(fin.real)
            fin_im.append(fin.imag)
        c_mat = lax.complex(c_re[dirn].astype(F32), c_im[dirn].astype(F32))
        y = y + jnp.einsum('blgp,gcp->blgc', hs, c_mat).real
    y = y.reshape(bsz, n, SSM_W).astype(u.dtype)
    if want_final:
        return y, jnp.stack(fin_re, axis=1), jnp.stack(fin_im, axis=1)
    return y, None, None


def _layer(x, mods, lp, angles, ctx_k, ctx_v, init_re, init_im):
    shift, scale, gate = mods
    bsz, n = x.shape[:2]
    h = _rmsnorm(x, lp['norm_g']) * (1.0 + scale) + shift
    z = h @ lp['w_in']
    cuts = np.cumsum(IN_SIZES)[:-1].tolist()
    conv_in, conv_gate, ssm_in, ssm_gate, q, k, v, attn_gate, merge = jnp.split(z, cuts, axis=-1)
    ya = _conformer_conv(conv_in, lp['conv_dw_w'], lp['conv_dw_b'], lp['conv_ln_g'], lp['conv_ln_b'])
    br_a = (ya * jax.nn.silu(conv_gate)) @ lp['w_conv_out']
    yb, fin_re, fin_im = _s5_bidirectional(
        ssm_in, lp['ssm_a_re'], lp['ssm_a_im'], lp['ssm_log_dt'], lp['ssm_b_re'], lp['ssm_b_im'],
        lp['ssm_c_re'], lp['ssm_c_im'], lp['ssm_d'], init_re, init_im)
    yg_a, yg_b = jnp.split(jax.nn.gelu(yb) @ lp['w_ssm_glu'], 2, axis=-1)
    yb = yg_a * jax.nn.sigmoid(yg_b)
    br_b = (yb * jax.nn.silu(ssm_gate)) @ lp['w_ssm_out']
    q = _rmsnorm(q.reshape(bsz, n, N_HEADS, HEAD_DIM), lp['q_norm_g'])
    k = _rmsnorm(k.reshape(bsz, n, N_KV_HEADS, HEAD_DIM), lp['k_norm_g'])
    v = v.reshape(bsz, n, N_KV_HEADS, HEAD_DIM)
    if ctx_k is None:
        k_all, v_all = k, v
    else:
        q = _axial_rope(q, angles)
        k_all = jnp.concatenate([_axial_rope(k, angles), ctx_k.astype(k.dtype)], axis=1)
        v_all = jnp.concatenate([v, ctx_v.astype(v.dtype)], axis=1)
    o = _blocked_attention(q, k_all, v_all)
    br_c = (o * jax.nn.silu(attn_gate)) @ lp['w_attn_out']
    g_a, g_b, g_c = jnp.split(jax.nn.sigmoid(merge), N_BRANCH, axis=-1)
    mixed = g_a * br_a + g_b * br_b + g_c * br_c
    x = x + gate * (mixed @ lp['w_out'])
    side = (k, v, fin_re, fin_im) if ctx_k is None else None
    return x, side


def setup_inputs(seed: int = 0) -> dict:
    key = jax.random.key(seed)
    ks = iter(jax.random.split(key, 48))

    def nrm(shape, scale):
        return jax.random.normal(next(ks), shape, F32) * scale

    n_idx = jnp.arange(SSM_P, dtype=F32)
    return {
        'x_prompt': nrm((BATCH, SEQ, D_MODEL), 1.0),
        'x_sample': nrm((DEC_BATCH, DEC_SEQ, D_MODEL), 1.0),
        'cache_k': nrm((DEC_BATCH, DEPTH, PAST_LEN, N_KV_HEADS, HEAD_DIM), 1.0),
        'cache_v': nrm((DEC_BATCH, DEPTH, PAST_LEN, N_KV_HEADS, HEAD_DIM), 1.0),
        'state_ssm_re': nrm((DEC_BATCH, DEPTH, N_DIR, SSM_G, SSM_P), 0.5),
        'state_ssm_im': nrm((DEC_BATCH, DEPTH, N_DIR, SSM_G, SSM_P), 0.5),
        'c': nrm((DEC_BATCH, D_MODEL), 1.0),
        'c_ctx': nrm((D_MODEL,), 1.0),
        'norm_g': 1.0 + nrm((DEPTH, D_MODEL), 0.02),
        'w_ada': nrm((DEPTH, D_MODEL, 3 * D_MODEL), 0.5 * D_MODEL ** -0.5),
        'b_ada': nrm((DEPTH, 3 * D_MODEL), 0.01),
        'w_in': nrm((DEPTH, D_MODEL, IN_COLS), D_MODEL ** -0.5),
        'conv_dw_w': nrm((DEPTH, CONV_K, CONV_W), CONV_K ** -0.5),
        'conv_dw_b': nrm((DEPTH, CONV_W), 0.01),
        'conv_ln_g': 1.0 + nrm((DEPTH, CONV_W), 0.02),
        'conv_ln_b': nrm((DEPTH, CONV_W), 0.01),
        'w_conv_out': nrm((DEPTH, CONV_W, D_MODEL), CONV_W ** -0.5),
        'ssm_a_re': -0.5 + nrm((DEPTH, N_DIR, SSM_G, SSM_P), 0.01),
        'ssm_a_im': math.pi * n_idx + nrm((DEPTH, N_DIR, SSM_G, SSM_P), 0.01),
        'ssm_log_dt': jax.random.uniform(next(ks), (DEPTH, N_DIR, SSM_G), F32, math.log(1e-3), math.log(1e-1)),
        'ssm_b_re': nrm((DEPTH, N_DIR, SSM_G, SSM_P, SSM_GC), (2 * SSM_GC) ** -0.5),
        'ssm_b_im': nrm((DEPTH, N_DIR, SSM_G, SSM_P, SSM_GC), (2 * SSM_GC) ** -0.5),
        'ssm_c_re': nrm((DEPTH, N_DIR, SSM_G, SSM_GC, SSM_P), (2 * SSM_P) ** -0.5),
        'ssm_c_im': nrm((DEPTH, N_DIR, SSM_G, SSM_GC, SSM_P), (2 * SSM_P) ** -0.5),
        'ssm_d': nrm((DEPTH, SSM_W), 1.0),
        'w_ssm_glu': nrm((DEPTH, SSM_W, 2 * SSM_W), SSM_W ** -0.5),
        'w_ssm_out': nrm((DEPTH, SSM_W, D_MODEL), SSM_W ** -0.5),
        'q_norm_g': 1.0 + nrm((DEPTH, HEAD_DIM), 0.02),
        'k_norm_g': 1.0 + nrm((DEPTH, HEAD_DIM), 0.02),
        'w_attn_out': nrm((DEPTH, ATTN_W, D_MODEL), ATTN_W ** -0.5),
        'w_out': nrm((DEPTH, D_MODEL, D_MODEL), D_MODEL ** -0.5),
    }


def reference(x_prompt, x_sample, cache_k, cache_v, state_ssm_re, state_ssm_im, c, c_ctx,
              norm_g, w_ada, b_ada, w_in, conv_dw_w, conv_dw_b, conv_ln_g, conv_ln_b, w_conv_out,
              ssm_a_re, ssm_a_im, ssm_log_dt, ssm_b_re, ssm_b_im, ssm_c_re, ssm_c_im, ssm_d,
              w_ssm_glu, w_ssm_out, q_norm_g, k_norm_g, w_attn_out, w_out):
    angles = _grid_angles(x_sample.shape[1])
    yp = x_prompt
    ys = x_sample
    ks_out, vs_out, re_out, im_out = [], [], [], []
    for l in range(DEPTH):
        lp = dict(norm_g=norm_g[l], w_in=w_in[l], conv_dw_w=conv_dw_w[l], conv_dw_b=conv_dw_b[l],
                  conv_ln_g=conv_ln_g[l], conv_ln_b=conv_ln_b[l], w_conv_out=w_conv_out[l],
                  ssm_a_re=ssm_a_re[l], ssm_a_im=ssm_a_im[l], ssm_log_dt=ssm_log_dt[l],
                  ssm_b_re=ssm_b_re[l], ssm_b_im=ssm_b_im[l], ssm_c_re=ssm_c_re[l], ssm_c_im=ssm_c_im[l],
                  ssm_d=ssm_d[l], w_ssm_glu=w_ssm_glu[l], w_ssm_out=w_ssm_out[l],
                  q_norm_g=q_norm_g[l], k_norm_g=k_norm_g[l], w_attn_out=w_attn_out[l], w_out=w_out[l])
        ctx_mods = _adaln(c_ctx[None, :], w_ada[l], b_ada[l])
        yp, side = _layer(yp, ctx_mods, lp, None, None, None, None, None)
        k_l, v_l, re_l, im_l = side
        ks_out.append(k_l)
        vs_out.append(v_l)
        re_out.append(re_l)
        im_out.append(im_l)
        lat_mods = _adaln(c, w_ada[l], b_ada[l])
        ys, _ = _layer(ys, lat_mods, lp, angles, cache_k[:, l], cache_v[:, l],
                       state_ssm_re[:, l], state_ssm_im[:, l])
    new_cache_k = jnp.stack(ks_out, axis=1)
    new_cache_v = jnp.stack(vs_out, axis=1)
    new_state_ssm_re = jnp.stack(re_out, axis=1)
    new_state_ssm_im = jnp.stack(im_out, axis=1)
    return (yp, ys, new_cache_k, new_cache_v, new_state_ssm_re, new_state_ssm_im)
```

```python
import functools
import math

import jax
import jax.numpy as jnp
import numpy as np
from jax import lax
from jax.experimental import pallas as pl
from jax.experimental.pallas import tpu as pltpu

F32 = jnp.float32
BF16 = jnp.bfloat16

D_MODEL = 1024
CONV_W = 512
CONV_K = 31
SSM_W = 512
SSM_GC = 16
SSM_G = 32
SSM_P = 64
N_HEADS = 8
N_KV_HEADS = 2
HEAD_DIM = 64
GQ = N_HEADS // N_KV_HEADS
ATTN_W = N_HEADS * HEAD_DIM
KV_W = N_KV_HEADS * HEAD_DIM
ROPE_PAIRS = HEAD_DIM // 4
ROPE_BASE = 10000.0
GRID_W = 64
EPS = 1e-6

C_CONV = 0
C_CGATE = 1024
C_U = 1536
C_UGATE = 2048
C_QKV = 2560
C_AGATE = 3328
C_MERGE = 3840
IN_COLS = 6912

SUBLANES = 8
LANES = 128
N_SEG = SUBLANES
SCAN_GROUPS = 4
N_CHUNK = SSM_G // SCAN_GROUPS
CHUNK_STATES = SCAN_GROUPS * SSM_P
SCAN_COLS = 4 * CHUNK_STATES
CONV_TILE = 32
CONV_HALO = 16
ROW_CHUNK = 256
VMEM_PHYSICAL_BYTES = 64 * 1024 * 1024


def _sigmoid(x):
    return jax.nn.sigmoid(x)


def _silu(x):
    return x * jax.nn.sigmoid(x)


def _gelu_tanh(x):
    return 0.5 * x * (1.0 + jnp.tanh(math.sqrt(2.0 / math.pi) * (x + 0.044715 * (x * x * x))))


def _dot(a, b):
    return jnp.dot(a, b, preferred_element_type=F32)


def _adaln_kernel(cond_ref, w_ref, b_ref, o_ref):
    s = _silu(cond_ref[...])
    o_ref[0] = jnp.dot(s, w_ref[0], preferred_element_type=F32,
                       precision=lax.Precision.HIGHEST) + b_ref[0]


def _adaln(cond, w_ada, b_ada):
    depth, d, n = w_ada.shape
    tn = 512
    return pl.pallas_call(
        _adaln_kernel,
        grid=(depth, n // tn),
        in_specs=[
            pl.BlockSpec((SUBLANES, d), lambda l, j: (0, 0)),
            pl.BlockSpec((1, d, tn), lambda l, j: (l, 0, j)),
            pl.BlockSpec((1, 1, tn), lambda l, j: (l, 0, j)),
        ],
        out_specs=pl.BlockSpec((1, SUBLANES, tn), lambda l, j: (l, 0, j)),
        out_shape=jax.ShapeDtypeStruct((depth, SUBLANES, n), F32),
        name="adaln",
    )(cond, w_ada, b_ada.reshape(depth, 1, n))


def _layer_kernel(cfg, *refs):
    R, S, L, q, nseq, latent, past = (cfg[k] for k in ("R", "S", "L", "q", "nseq", "latent", "past"))
    lk = S + past
    n_rc = R // ROW_CHUNK
    it = iter(refs)
    x_ref, mods_ref, ng_ref, w_in_ref, cw_ref, cvec_ref = (next(it) for _ in range(6))
    w_conv_out_ref, bm_ref, cm_ref, sp_ref, w_glu_ref, w_ssm_out_ref = (next(it) for _ in range(6))
    ones_ref, w_attn_out_ref, w_out_ref = (next(it) for _ in range(3))
    if latent:
        cos_ref, sin_ref, ck_ref, cvv_ref, h0re_ref, h0im_ref = (next(it) for _ in range(6))
    out_ref = next(it)
    if not latent:
        k_out_ref, v_out_ref, st_re_ref, st_im_ref = (next(it) for _ in range(4))
    h_scr, ypad_scr, sl_scr, big_scr, kall_scr, vall_scr = (next(it) for _ in range(6))

    ng = ng_ref[...]
    shift = mods_ref[0, 0:1, :]
    scale1 = 1.0 + mods_ref[0, 1:2, :]
    gate = mods_ref[0, 2:3, :]
    pitch = S + 2 * CONV_HALO

    def rows(rc):
        return slice(rc * ROW_CHUNK, (rc + 1) * ROW_CHUNK)

    zero_halo = jnp.zeros((CONV_HALO, CONV_W), F32)
    for sq in range(nseq):
        ypad_scr[sq * pitch:sq * pitch + CONV_HALO, :] = zero_halo
        ypad_scr[sq * pitch + CONV_HALO + S:(sq + 1) * pitch, :] = zero_halo
    for rc in range(n_rc):
        x = x_ref[rows(rc), :]
        ms = jnp.mean(x * x, axis=-1, keepdims=True)
        hn = (x * lax.rsqrt(ms + EPS) * ng) * scale1 + shift
        hb = hn.astype(BF16)
        h_scr[rows(rc), :] = hb
        zc = _dot(hb, w_in_ref[:, C_CONV:C_CONV + 2 * CONV_W])
        y = zc[:, :CONV_W] * _sigmoid(zc[:, CONV_W:])
        r0 = rc * ROW_CHUNK
        sq = r0 // S
        p0 = sq * pitch + CONV_HALO + (r0 - sq * S)
        ypad_scr[p0:p0 + ROW_CHUNK, :] = y

    conv_b = cvec_ref[0:1, :]
    ln_g = cvec_ref[1:2, :]
    ln_b = cvec_ref[2:3, :]
    log2_s = int(math.log2(S))

    def conv_tile(t, carry):
        r0 = pl.multiple_of(t * CONV_TILE, CONV_TILE)
        sq = lax.shift_right_logical(t * CONV_TILE, log2_s)
        w0 = pl.multiple_of(r0 + sq * (2 * CONV_HALO), CONV_TILE)
        win = ypad_scr[pl.ds(w0, 2 * CONV_TILE), :]
        acc = jnp.broadcast_to(conv_b, (CONV_TILE, CONV_W))
        for res in range(SUBLANES):
            rw = win if res == 0 else pltpu.roll(win, 2 * CONV_TILE - res, 0)
            for off in range(res, CONV_K + 1, SUBLANES):
                if off == 0:
                    continue
                acc = acc + cw_ref[off - 1:off, :] * rw[off - res:off - res + CONV_TILE, :]
        mu = jnp.mean(acc, axis=-1, keepdims=True)
        xc = acc - mu
        var = jnp.mean(xc * xc, axis=-1, keepdims=True)
        yl = xc * lax.rsqrt(var + EPS) * ln_g + ln_b
        big_scr[pl.ds(r0, CONV_TILE), 0:CONV_W] = _silu(yl)
        return carry

    lax.fori_loop(0, R // CONV_TILE, conv_tile, 0)

    for rc in range(n_rc):
        hb = h_scr[rows(rc), :]
        cg = _dot(hb, w_in_ref[:, C_CGATE:C_CGATE + CONV_W])
        t = (big_scr[rows(rc), 0:CONV_W] * _silu(cg)).astype(BF16)
        br = _dot(t, w_conv_out_ref[...])
        g = _sigmoid(_dot(hb, w_in_ref[:, C_MERGE:C_MERGE + D_MODEL]))
        out_ref[rows(rc), :] = g * br

    n_slab = SSM_W // LANES
    for rc in range(n_rc):
        u = _dot(h_scr[rows(rc), :], w_in_ref[:, C_U:C_U + SSM_W])
        for s4 in range(n_slab):
            sl_scr[s4, rows(rc), :] = u[:, s4 * LANES:(s4 + 1) * LANES]

    def permute_rows(i, carry):
        dst = pl.multiple_of(i * N_SEG, N_SEG)
        for s4 in range(n_slab):
            sl_scr[n_slab + s4, pl.ds(dst, N_SEG), :] = sl_scr[s4, pl.ds(i, N_SEG, stride=L), :]
        return carry

    lax.fori_loop(0, L, permute_rows, 0)

    tile = (N_SEG, CHUNK_STATES)
    seg_in_seq = lax.broadcasted_iota(jnp.int32, tile, 0) & (q - 1)
    c0, c1, c2, c3, c4 = (i * CHUNK_STATES for i in range(5))
    last = (L - 1) * N_SEG

    def cmul(ar, ai, br_, bi_):
        return ar * br_ - ai * bi_, ar * bi_ + ai * br_

    for c in range(N_CHUNK):
        slab, half = divmod(c, 2)
        for rc in range(n_rc):
            big_scr[rows(rc), :] = _dot(sl_scr[n_slab + slab, rows(rc), :].astype(BF16), bm_ref[c])

        def sp_row(i, d):
            return jnp.broadcast_to(sp_ref[c, i:i + 1, d * CHUNK_STATES:(d + 1) * CHUNK_STATES], tile)

        lfr, lfi, lbr, lbi = sp_row(0, 0), sp_row(1, 0), sp_row(0, 1), sp_row(1, 1)

        def scan_step(i, carry):
            hfr, hfi, hbr, hbi = carry
            rf = pl.multiple_of(i * N_SEG, N_SEG)
            rb = pl.multiple_of((L - 1 - i) * N_SEG, N_SEG)
            nfr = lfr * hfr - lfi * hfi + big_scr[pl.ds(rf, N_SEG), c0:c1]
            nfi = lfr * hfi + lfi * hfr + big_scr[pl.ds(rf, N_SEG), c1:c2]
            nbr = lbr * hbr - lbi * hbi + big_scr[pl.ds(rb, N_SEG), c2:c3]
            nbi = lbr * hbi + lbi * hbr + big_scr[pl.ds(rb, N_SEG), c3:c4]
            big_scr[pl.ds(rf, N_SEG), c0:c1] = nfr
            big_scr[pl.ds(rf, N_SEG), c1:c2] = nfi
            big_scr[pl.ds(rb, N_SEG), c2:c3] = nbr
            big_scr[pl.ds(rb, N_SEG), c3:c4] = nbi
            return nfr, nfi, nbr, nbi

        z = jnp.zeros(tile, F32)
        lax.fori_loop(0, L, scan_step, (z, z, z, z), unroll=4)

        efr, efi = big_scr[last:last + N_SEG, c0:c1], big_scr[last:last + N_SEG, c1:c2]
        ebr, ebi = big_scr[0:N_SEG, c2:c3], big_scr[0:N_SEG, c3:c4]
        if latent:
            h0 = [jnp.broadcast_to(r[0, d, c:c + 1, :], tile) for d in range(2) for r in (h0re_ref, h0im_ref)]
        else:
            h0 = [z, z, z, z]
        hfr = jnp.where(seg_in_seq == 0, h0[0], pltpu.roll(efr, 1, 0))
        hfi = jnp.where(seg_in_seq == 0, h0[1], pltpu.roll(efi, 1, 0))
        hbr = jnp.where(seg_in_seq == q - 1, h0[2], pltpu.roll(ebr, N_SEG - 1, 0))
        hbi = jnp.where(seg_in_seq == q - 1, h0[3], pltpu.roll(ebi, N_SEG - 1, 0))
        for j, dist in enumerate((1, 2, 4)):
            if dist >= q:
                break
            pfr, pfi, pbr, pbi = sp_row(2 + 2 * j, 0), sp_row(3 + 2 * j, 0), sp_row(2 + 2 * j, 1), sp_row(3 + 2 * j, 1)
            ar, ai = cmul(pfr, pfi, pltpu.roll(hfr, dist, 0), pltpu.roll(hfi, dist, 0))
            keep = seg_in_seq >= dist
            hfr, hfi = hfr + jnp.where(keep, ar, 0.0), hfi + jnp.where(keep, ai, 0.0)
            ar, ai = cmul(pbr, pbi, pltpu.roll(hbr, N_SEG - dist, 0), pltpu.roll(hbi, N_SEG - dist, 0))
            keep = seg_in_seq <= q - 1 - dist
            hbr, hbi = hbr + jnp.where(keep, ar, 0.0), hbi + jnp.where(keep, ai, 0.0)

        if not latent:
            plr, pli = sp_row(2, 0), sp_row(3, 0)
            ffr, ffi = cmul(plr, pli, hfr, hfi)
            ffr, ffi = ffr + efr, ffi + efi
            plr, pli = sp_row(2, 1), sp_row(3, 1)
            fbr, fbi = cmul(plr, pli, hbr, hbi)
            fbr, fbi = fbr + ebr, fbi + ebi
            cols = slice(c * CHUNK_STATES, (c + 1) * CHUNK_STATES)
            for sq in range(nseq):
                lf = (sq + 1) * q - 1
                lb = sq * q
                st_re_ref[sq, 0:1, cols] = ffr[lf:lf + 1, :]
                st_im_ref[sq, 0:1, cols] = ffi[lf:lf + 1, :]
                st_re_ref[sq, 1:2, cols] = fbr[lb:lb + 1, :]
                st_im_ref[sq, 1:2, cols] = fbi[lb:lb + 1, :]

        def stitch_step(i, carry):
            pfr, pfi, pbr, pbi = carry
            rf = pl.multiple_of(i * N_SEG, N_SEG)
            rb = pl.multiple_of((L - 1 - i) * N_SEG, N_SEG)
            ar, ai = cmul(pfr, pfi, hfr, hfi)
            big_scr[pl.ds(rf, N_SEG), c0:c1] += ar
            big_scr[pl.ds(rf, N_SEG), c1:c2] += ai
            ar, ai = cmul(pbr, pbi, hbr, hbi)
            big_scr[pl.ds(rb, N_SEG), c2:c3] += ar
            big_scr[pl.ds(rb, N_SEG), c3:c4] += ai
            nfr, nfi = cmul(pfr, pfi, lfr, lfi)
            nbr, nbi = cmul(pbr, pbi, lbr, lbi)
            return nfr, nfi, nbr, nbi

        lax.fori_loop(0, L, stitch_step, (lfr, lfi, lbr, lbi), unroll=2)

        for rc in range(n_rc):
            yc = _dot(big_scr[rows(rc), :].astype(BF16), cm_ref[c])
            if half == 0:
                sl_scr[2 * n_slab + slab, rows(rc), :] = yc
            else:
                sl_scr[2 * n_slab + slab, rows(rc), :] += yc

    def unpermute_rows(j, carry):
        src = j * (N_SEG * SUBLANES)
        for k in range(N_SEG):
            dst = pl.multiple_of(k * L + j * SUBLANES, SUBLANES)
            for s4 in range(n_slab):
                big_scr[pl.ds(dst, SUBLANES), s4 * LANES:(s4 + 1) * LANES] = (
                    sl_scr[2 * n_slab + s4, pl.ds(src + k, SUBLANES, stride=N_SEG), :])
        return carry

    lax.fori_loop(0, L // SUBLANES, unpermute_rows, 0)

    ssm_d = cvec_ref[3:4, :]
    for rc in range(n_rc):
        hb = h_scr[rows(rc), :]
        u = jnp.concatenate([sl_scr[s4, rows(rc), :] for s4 in range(n_slab)], axis=1)
        yb = big_scr[rows(rc), 0:SSM_W] + ssm_d * u
        z2 = _dot(_gelu_tanh(yb).astype(BF16), w_glu_ref[...])
        yb = z2[:, :SSM_W] * _sigmoid(z2[:, SSM_W:])
        sg = _dot(hb, w_in_ref[:, C_UGATE:C_UGATE + SSM_W])
        t = (yb * _silu(sg)).astype(BF16)
        br = _dot(t, w_ssm_out_ref[...])
        g = _sigmoid(_dot(hb, w_in_ref[:, C_MERGE + D_MODEL:C_MERGE + 2 * D_MODEL]))
        out_ref[rows(rc), :] += g * br

    q_g = cvec_ref[4:5, :]
    k_g = cvec_ref[5:6, 0:KV_W]
    lane = lax.broadcasted_iota(jnp.int32, (ROW_CHUNK, LANES), 1)
    first_half = (lane & (2 * ROPE_PAIRS - 1)) < ROPE_PAIRS

    def rope(xs, cos, sin):
        swapped = jnp.where(first_half, pltpu.roll(xs, LANES - ROPE_PAIRS, 1), pltpu.roll(xs, ROPE_PAIRS, 1))
        return xs * cos + swapped * sin

    def head_ms(x2, n):
        hi = x2.astype(BF16)
        lo = (x2 - hi.astype(F32)).astype(BF16)
        ones = ones_ref[0:n, 0:n]
        return (_dot(hi, ones) + _dot(lo, ones)) * (1.0 / HEAD_DIM)

    for rc in range(n_rc):
        zq = _dot(h_scr[rows(rc), :], w_in_ref[:, C_QKV:C_QKV + ATTN_W + 2 * KV_W])
        qf, kf, vf = zq[:, :ATTN_W], zq[:, ATTN_W:ATTN_W + KV_W], zq[:, ATTN_W + KV_W:]
        qn = qf * lax.rsqrt(head_ms(qf * qf, ATTN_W) + EPS) * q_g
        kn = kf * lax.rsqrt(head_ms(kf * kf, KV_W) + EPS) * k_g
        r0 = rc * ROW_CHUNK
        sq = r0 // S
        t0 = r0 - sq * S
        if latent:
            cos = cos_ref[t0:t0 + ROW_CHUNK, :]
            sin = sin_ref[t0:t0 + ROW_CHUNK, :]
            qn = jnp.concatenate(
                [rope(qn[:, i * LANES:(i + 1) * LANES], cos, sin) for i in range(ATTN_W // LANES)], axis=1)
            kn = rope(kn, cos, sin)
        else:
            k_out_ref[rows(rc), :] = kn
            v_out_ref[rows(rc), :] = vf
        big_scr[rows(rc), 0:ATTN_W] = qn * (HEAD_DIM ** -0.5)
        kall_scr[sq, t0:t0 + ROW_CHUNK, :] = kn
        vall_scr[sq, t0:t0 + ROW_CHUNK, :] = vf
    if latent:
        kall_scr[0, S:lk, :] = ck_ref[0]
        vall_scr[0, S:lk, :] = cvv_ref[0]

    def attend(sq, row0):
        for hd in range(N_HEADS):
            kv = hd // GQ
            qh = big_scr[pl.ds(row0, ROW_CHUNK), hd * HEAD_DIM:(hd + 1) * HEAD_DIM].astype(BF16)
            kh = kall_scr[sq, :, kv * HEAD_DIM:(kv + 1) * HEAD_DIM].astype(BF16)
            vh = vall_scr[sq, :, kv * HEAD_DIM:(kv + 1) * HEAD_DIM].astype(BF16)
            s = lax.dot_general(qh, kh, (((1,), (1,)), ((), ())), preferred_element_type=F32)
            e = jnp.exp(s - jnp.max(s, axis=-1, keepdims=True))
            o = _dot(e.astype(BF16), vh) / jnp.sum(e, axis=-1, keepdims=True)
            big_scr[pl.ds(row0, ROW_CHUNK), ATTN_W + hd * HEAD_DIM:ATTN_W + (hd + 1) * HEAD_DIM] = o

    if latent:
        def attend_chunk(qc, carry):
            attend(0, pl.multiple_of(qc * ROW_CHUNK, ROW_CHUNK))
            return carry
        lax.fori_loop(0, n_rc, attend_chunk, 0)
    else:
        for sq in range(nseq):
            for qc in range(S // ROW_CHUNK):
                attend(sq, sq * S + qc * ROW_CHUNK)

    for rc in range(n_rc):
        hb = h_scr[rows(rc), :]
        ag = _dot(hb, w_in_ref[:, C_AGATE:C_AGATE + ATTN_W])
        t = (big_scr[rows(rc), ATTN_W:2 * ATTN_W] * _silu(ag)).astype(BF16)
        br = _dot(t, w_attn_out_ref[...])
        g = _sigmoid(_dot(hb, w_in_ref[:, C_MERGE + 2 * D_MODEL:C_MERGE + 3 * D_MODEL]))
        mixed = out_ref[rows(rc), :] + g * br
        out_ref[rows(rc), :] = x_ref[rows(rc), :] + gate * _dot(mixed.astype(BF16), w_out_ref[...])


def _const_spec(shape):
    nd = len(shape)
    return pl.BlockSpec(shape, lambda i, _nd=nd: (0,) * _nd, pipeline_mode=pl.Buffered(1))


def _layer_call(cfg, x2d, mods, lw, extra):
    R, S, nseq, latent, past = cfg["R"], cfg["S"], cfg["nseq"], cfg["latent"], cfg["past"]
    n_rows = x2d.shape[0]
    n_blocks = n_rows // R
    weights = [lw["ng"], lw["w_in"], lw["cw"], lw["cvec"], lw["w_conv_out"], lw["bm"], lw["cm"],
               lw["sp_lat" if latent else "sp_ctx"], lw["w_glu"], lw["w_ssm_out"], lw["ones"],
               lw["w_attn_out"], lw["w_out"]]
    if latent:
        mods_spec = pl.BlockSpec((1, SUBLANES, D_MODEL), lambda i: (i + 1, 0, 0))
    else:
        mods_spec = pl.BlockSpec((1, SUBLANES, D_MODEL), lambda i: (0, 0, 0))
    row_mode = dict(pipeline_mode=pl.Buffered(1)) if latent else {}
    in_specs = [pl.BlockSpec((R, D_MODEL), lambda i: (i, 0), **row_mode), mods_spec]
    in_specs += [_const_spec(w.shape) for w in weights]
    args = [x2d, mods] + weights
    out_shape = [jax.ShapeDtypeStruct((n_rows, D_MODEL), F32)]
    out_specs = [pl.BlockSpec((R, D_MODEL), lambda i: (i, 0), **row_mode)]
    if latent:
        cos, sin, ck, cv, h0re, h0im = extra
        in_specs += [_const_spec(cos.shape), _const_spec(sin.shape),
                     pl.BlockSpec((1, past, KV_W), lambda i: (i, 0, 0)),
                     pl.BlockSpec((1, past, KV_W), lambda i: (i, 0, 0)),
                     pl.BlockSpec((1, 2, N_CHUNK, CHUNK_STATES), lambda i: (i, 0, 0, 0)),
                     pl.BlockSpec((1, 2, N_CHUNK, CHUNK_STATES), lambda i: (i, 0, 0, 0))]
        args += [cos, sin, ck, cv, h0re, h0im]
    else:
        n_seq_total = n_rows // S
        out_shape += [jax.ShapeDtypeStruct((n_rows, KV_W), F32), jax.ShapeDtypeStruct((n_rows, KV_W), F32),
                      jax.ShapeDtypeStruct((n_seq_total, 2, SSM_G * SSM_P), F32),
                      jax.ShapeDtypeStruct((n_seq_total, 2, SSM_G * SSM_P), F32)]
        out_specs += [pl.BlockSpec((R, KV_W), lambda i: (i, 0)), pl.BlockSpec((R, KV_W), lambda i: (i, 0)),
                      pl.BlockSpec((nseq, 2, SSM_G * SSM_P), lambda i: (i, 0, 0)),
                      pl.BlockSpec((nseq, 2, SSM_G * SSM_P), lambda i: (i, 0, 0))]
    scratch = [
        pltpu.VMEM((R, D_MODEL), BF16),
        pltpu.VMEM((nseq * (S + 2 * CONV_HALO), CONV_W), F32),
        pltpu.VMEM((3 * SSM_W // LANES, R, LANES), F32),
        pltpu.VMEM((R, SCAN_COLS), F32),
        pltpu.VMEM((nseq, S + past, KV_W), F32),
        pltpu.VMEM((nseq, S + past, KV_W), F32),
    ]
    return pl.pallas_call(
        functools.partial(_layer_kernel, cfg),
        grid=(n_blocks,),
        in_specs=in_specs,
        out_specs=out_specs,
        out_shape=out_shape,
        scratch_shapes=scratch,
        compiler_params=pltpu.CompilerParams(
            dimension_semantics=("arbitrary",),
            vmem_limit_bytes=cfg["vmem_bytes"]),
        name="latent_layer" if latent else "context_layer",
    )(*args)


def _ssm_pack(a_re, a_im, log_dt, b_re, b_im, c_re, c_im, seg_len):
    lam = lax.complex(a_re, a_im)
    dt = jnp.exp(log_dt)[..., None]
    lam_bar = jnp.exp(lam * dt)
    b_bar = ((lam_bar - 1.0) / lam)[..., None] * lax.complex(b_re, b_im)
    eye = jnp.eye(SCAN_GROUPS, dtype=F32)
    bb = jnp.stack([b_bar.real, b_bar.imag], axis=1)
    bb = bb.reshape(2, 2, N_CHUNK, SCAN_GROUPS, SSM_P, SSM_GC)
    tb = jnp.einsum("dacgpk,hg->chkdagp", bb, eye).reshape(N_CHUNK, SCAN_GROUPS * SSM_GC, SCAN_COLS)
    zb = jnp.zeros_like(tb)
    odd = (jnp.arange(N_CHUNK) % 2 == 1)[:, None, None]
    bm = jnp.where(odd, jnp.concatenate([zb, tb], axis=1), jnp.concatenate([tb, zb], axis=1))
    cc = jnp.stack([c_re, -c_im], axis=1)
    cc = cc.reshape(2, 2, N_CHUNK, SCAN_GROUPS, SSM_GC, SSM_P)
    tc = jnp.einsum("dacgkp,hg->cdagphk", cc, eye).reshape(N_CHUNK, SCAN_COLS, SCAN_GROUPS * SSM_GC)
    zc = jnp.zeros_like(tc)
    cm = jnp.where(odd, jnp.concatenate([zc, tc], axis=2), jnp.concatenate([tc, zc], axis=2))

    def chunk_rows(v):
        return v.reshape(2, N_CHUNK, CHUNK_STATES).transpose(1, 0, 2).reshape(N_CHUNK, 2 * CHUNK_STATES)

    def sp_for(seg):
        pows = [lam_bar] + [jnp.exp(lam * dt * float(seg * d)) for d in (1, 2, 4)]
        rows_ = []
        for p in pows:
            rows_ += [chunk_rows(p.real), chunk_rows(p.imag)]
        return jnp.stack(rows_, axis=1)

    return bm.astype(BF16), cm.astype(BF16), [sp_for(s) for s in seg_len]


def _rope_tables(n_tok):
    rows = n_tok // GRID_W
    row = jnp.repeat(jnp.arange(rows, dtype=F32), GRID_W)
    col = jnp.tile(jnp.arange(GRID_W, dtype=F32), rows)
    inv = ROPE_BASE ** (-jnp.arange(ROPE_PAIRS, dtype=F32) / ROPE_PAIRS)
    ar, ac = row[:, None] * inv[None, :], col[:, None] * inv[None, :]
    ang = jnp.concatenate([ar, ar, ac, ac], axis=-1)
    sign = jnp.tile(jnp.concatenate([-jnp.ones(ROPE_PAIRS, F32), jnp.ones(ROPE_PAIRS, F32)]), 2)
    reps = LANES // HEAD_DIM
    return jnp.tile(jnp.cos(ang), (1, reps)), jnp.tile(jnp.sin(ang) * sign, (1, reps))


def kernel(x_prompt, x_sample, cache_k, cache_v, state_ssm_re, state_ssm_im, c, c_ctx, norm_g, w_ada, b_ada, w_in, conv_dw_w, conv_dw_b, conv_ln_g, conv_ln_b, w_conv_out, ssm_a_re, ssm_a_im, ssm_log_dt, ssm_b_re, ssm_b_im, ssm_c_re, ssm_c_im, ssm_d, w_ssm_glu, w_ssm_out, q_norm_g, k_norm_g, w_attn_out, w_out):
    batch, seq, d_model = x_prompt.shape
    dec_batch, dec_seq, _ = x_sample.shape
    depth = w_in.shape[0]
    past = cache_k.shape[2]
    assert d_model == D_MODEL and w_in.shape[2] == IN_COLS and dec_seq % GRID_W == 0

    ctx_cfg = dict(R=2 * seq, S=seq, nseq=2, latent=False, past=0, vmem_bytes=52 * 1024 * 1024)
    lat_cfg = dict(R=dec_seq, S=dec_seq, nseq=1, latent=True, past=past, vmem_bytes=60 * 1024 * 1024)
    for cfg in (ctx_cfg, lat_cfg):
        cfg["L"] = cfg["R"] // N_SEG
        cfg["q"] = cfg["S"] // cfg["L"]
        assert cfg["R"] % ROW_CHUNK == 0 and cfg["S"] % ROW_CHUNK == 0 and cfg["q"] in (1, 2, 4, 8)
        assert cfg["vmem_bytes"] <= VMEM_PHYSICAL_BYTES

    cond = jnp.zeros((SUBLANES, d_model), F32).at[0].set(c_ctx).at[1:1 + dec_batch].set(c)
    mods_all = _adaln(cond, w_ada, b_ada)
    cos, sin = _rope_tables(dec_seq)
    head_id = jnp.arange(ATTN_W) // HEAD_DIM
    ones = (head_id[:, None] == head_id[None, :]).astype(BF16)

    yp = x_prompt.reshape(batch * seq, d_model)
    ys = x_sample.reshape(dec_batch * dec_seq, d_model)
    ks_out, vs_out, re_out, im_out = [], [], [], []
    for l in range(depth):
        bm, cm, (sp_ctx, sp_lat) = _ssm_pack(
            ssm_a_re[l], ssm_a_im[l], ssm_log_dt[l], ssm_b_re[l], ssm_b_im[l], ssm_c_re[l], ssm_c_im[l],
            (ctx_cfg["L"], lat_cfg["L"]))
        cvec = jnp.zeros((SUBLANES, CONV_W), F32)
        cvec = cvec.at[0].set(conv_dw_b[l]).at[1].set(conv_ln_g[l]).at[2].set(conv_ln_b[l]).at[3].set(ssm_d[l])
        cvec = cvec.at[4].set(jnp.tile(q_norm_g[l], N_HEADS)).at[5].set(jnp.tile(k_norm_g[l], N_HEADS))
        lw = dict(
            ng=norm_g[l].reshape(1, d_model), w_in=w_in[l].astype(BF16),
            cw=jnp.zeros((CONV_K + 1, CONV_W), F32).at[:CONV_K].set(conv_dw_w[l]), cvec=cvec,
            w_conv_out=w_conv_out[l].astype(BF16), bm=bm, cm=cm, sp_ctx=sp_ctx, sp_lat=sp_lat,
            w_glu=w_ssm_glu[l].astype(BF16), w_ssm_out=w_ssm_out[l].astype(BF16), ones=ones,
            w_attn_out=w_attn_out[l].astype(BF16), w_out=w_out[l].astype(BF16))
        m3 = mods_all[l, :1 + dec_batch].reshape(1 + dec_batch, 3, d_model)
        mods = jnp.zeros((1 + dec_batch, SUBLANES, d_model), F32).at[:, :3].set(m3)

        yp, k_l, v_l, re_l, im_l = _layer_call(ctx_cfg, yp, mods, lw, None)
        ks_out.append(k_l.reshape(batch, seq, N_KV_HEADS, HEAD_DIM))
        vs_out.append(v_l.reshape(batch, seq, N_KV_HEADS, HEAD_DIM))
        re_out.append(re_l.reshape(batch, 2, SSM_G, SSM_P))
        im_out.append(im_l.reshape(batch, 2, SSM_G, SSM_P))

        extra = (cos, sin, cache_k[:, l].reshape(dec_batch, past, KV_W), cache_v[:, l].reshape(dec_batch, past, KV_W),
                 state_ssm_re[:, l].reshape(dec_batch, 2, N_CHUNK, CHUNK_STATES),
                 state_ssm_im[:, l].reshape(dec_batch, 2, N_CHUNK, CHUNK_STATES))
        (ys,) = _layer_call(lat_cfg, ys, mods, lw, extra)

    return (yp.reshape(batch, seq, d_model), ys.reshape(dec_batch, dec_seq, d_model),
            jnp.stack(ks_out, axis=1), jnp.stack(vs_out, axis=1),
            jnp.stack(re_out, axis=1), jnp.stack(im_out, axis=1))
```

```python
import functools
import math

import jax
import jax.numpy as jnp
import numpy as np
from jax import lax
from jax.experimental import pallas as pl
from jax.experimental.pallas import tpu as pltpu

F32 = jnp.float32
BF16 = jnp.bfloat16

D_MODEL = 1024
CONV_W = 512
CONV_K = 31
SSM_W = 512
SSM_GC = 16
SSM_G = 32
SSM_P = 64
N_HEADS = 8
N_KV_HEADS = 2
HEAD_DIM = 64
GQ = N_HEADS // N_KV_HEADS
ATTN_W = N_HEADS * HEAD_DIM
KV_W = N_KV_HEADS * HEAD_DIM
ROPE_PAIRS = HEAD_DIM // 4
ROPE_BASE = 10000.0
GRID_W = 64
EPS = 1e-6

C_CONV = 0
C_CGATE = 1024
C_U = 1536
C_UGATE = 2048
C_QKV = 2560
C_AGATE = 3328
C_MERGE = 3840
IN_COLS = 6912

SUBLANES = 8
LANES = 128
N_SEG = SUBLANES
SEG_PAD = 8
SCAN_GROUPS = 4
N_CHUNK = SSM_G // SCAN_GROUPS
CHUNK_STATES = SCAN_GROUPS * SSM_P
SCAN_COLS = 4 * CHUNK_STATES
CONV_TILE = 32
CONV_HALO = 16
ROW_CHUNK = 256
VMEM_PHYSICAL_BYTES = 64 * 1024 * 1024


def _sigmoid(x):
    return jax.nn.sigmoid(x)


def _silu(x):
    return x * jax.nn.sigmoid(x)


def _gelu_tanh(x):
    return 0.5 * x * (1.0 + jnp.tanh(math.sqrt(2.0 / math.pi) * (x + 0.044715 * (x * x * x))))


def _dot(a, b):
    return jnp.dot(a, b, preferred_element_type=F32)


def _adaln_kernel(cond_ref, w_ref, b_ref, o_ref):
    s = _silu(cond_ref[...])
    o_ref[0] = jnp.dot(s, w_ref[0], preferred_element_type=F32,
                       precision=lax.Precision.HIGHEST) + b_ref[0]


def _adaln(cond, w_ada, b_ada):
    depth, d, n = w_ada.shape
    tn = 512
    return pl.pallas_call(
        _adaln_kernel,
        grid=(depth, n // tn),
        in_specs=[
            pl.BlockSpec((SUBLANES, d), lambda l, j: (0, 0)),
            pl.BlockSpec((1, d, tn), lambda l, j: (l, 0, j)),
            pl.BlockSpec((1, 1, tn), lambda l, j: (l, 0, j)),
        ],
        out_specs=pl.BlockSpec((1, SUBLANES, tn), lambda l, j: (l, 0, j)),
        out_shape=jax.ShapeDtypeStruct((depth, SUBLANES, n), F32),
        name="adaln",
    )(cond, w_ada, b_ada.reshape(depth, 1, n))


def _layer_kernel(cfg, *refs):
    R, S, L, q, nseq, latent, past = (cfg[k] for k in ("R", "S", "L", "q", "nseq", "latent", "past"))
    lk = S + past
    n_rc = R // ROW_CHUNK
    it = iter(refs)
    x_ref, mods_ref, ng_ref, w_in_ref, cw_ref, cvec_ref = (next(it) for _ in range(6))
    w_conv_out_ref, bm_ref, cm_ref, sp_ref, w_glu_ref, w_ssm_out_ref = (next(it) for _ in range(6))
    ones_ref, w_attn_out_ref, w_out_ref = (next(it) for _ in range(3))
    if latent:
        cos_ref, sin_ref, ck_ref, cvv_ref, h0re_ref, h0im_ref = (next(it) for _ in range(6))
    out_ref = next(it)
    if not latent:
        k_out_ref, v_out_ref, st_re_ref, st_im_ref = (next(it) for _ in range(4))
    h_scr, ypad_scr, u_scr, up_scr, yp_scr, big_scr, kall_scr, vall_scr = (next(it) for _ in range(8))
    n_slab = SSM_W // LANES
    seg_pitch = L + SEG_PAD

    ng = ng_ref[...]
    shift = mods_ref[0, 0:1, :]
    scale1 = 1.0 + mods_ref[0, 1:2, :]
    gate = mods_ref[0, 2:3, :]
    pitch = S + 2 * CONV_HALO

    def rows(rc):
        return slice(rc * ROW_CHUNK, (rc + 1) * ROW_CHUNK)

    zero_halo = jnp.zeros((CONV_HALO, LANES), F32)
    for sq in range(nseq):
        for s4 in range(n_slab):
            ypad_scr[s4, sq * pitch:sq * pitch + CONV_HALO, :] = zero_halo
            ypad_scr[s4, sq * pitch + CONV_HALO + S:(sq + 1) * pitch, :] = zero_halo
    for rc in range(n_rc):
        x = x_ref[rows(rc), :]
        ms = jnp.mean(x * x, axis=-1, keepdims=True)
        hn = (x * lax.rsqrt(ms + EPS) * ng) * scale1 + shift
        hb = hn.astype(BF16)
        h_scr[rows(rc), :] = hb
        zc = _dot(hb, w_in_ref[:, C_CONV:C_CONV + 2 * CONV_W])
        y = zc[:, :CONV_W] * _sigmoid(zc[:, CONV_W:])
        r0 = rc * ROW_CHUNK
        sq = r0 // S
        p0 = sq * pitch + CONV_HALO + (r0 - sq * S)
        for s4 in range(n_slab):
            ypad_scr[s4, p0:p0 + ROW_CHUNK, :] = y[:, s4 * LANES:(s4 + 1) * LANES]

    conv_b = cvec_ref[0:1, :]
    ln_g = cvec_ref[1:2, :]
    ln_b = cvec_ref[2:3, :]
    log2_s = int(math.log2(S))

    def conv_tile(t, carry):
        r0 = pl.multiple_of(t * CONV_TILE, CONV_TILE)
        sq = lax.shift_right_logical(t * CONV_TILE, log2_s)
        w0 = pl.multiple_of(r0 + sq * (2 * CONV_HALO), CONV_TILE)
        accs = []
        for s4 in range(n_slab):
            lanes = slice(s4 * LANES, (s4 + 1) * LANES)
            a = jnp.broadcast_to(conv_b[:, lanes], (CONV_TILE, LANES))
            for k in range(CONV_K):
                a = a + cw_ref[k:k + 1, lanes] * ypad_scr[s4, pl.ds(w0 + 1 + k, CONV_TILE, stride=1), :]
            accs.append(a)
        acc = jnp.concatenate(accs, axis=1)
        mu = jnp.mean(acc, axis=-1, keepdims=True)
        xc = acc - mu
        var = jnp.mean(xc * xc, axis=-1, keepdims=True)
        yl = xc * lax.rsqrt(var + EPS) * ln_g + ln_b
        big_scr[pl.ds(r0, CONV_TILE), 0:CONV_W] = _silu(yl)
        return carry

    lax.fori_loop(0, R // CONV_TILE, conv_tile, 0)

    for rc in range(n_rc):
        hb = h_scr[rows(rc), :]
        cg = _dot(hb, w_in_ref[:, C_CGATE:C_CGATE + CONV_W])
        t = (big_scr[rows(rc), 0:CONV_W] * _silu(cg)).astype(BF16)
        br = _dot(t, w_conv_out_ref[...])
        g = _sigmoid(_dot(hb, w_in_ref[:, C_MERGE:C_MERGE + D_MODEL]))
        out_ref[rows(rc), :] = g * br

    segs_per_chunk = ROW_CHUNK // L

    def seg_rows(rc, j):
        k = rc * segs_per_chunk + j
        return slice(k * seg_pitch, k * seg_pitch + L)

    for rc in range(n_rc):
        u = _dot(h_scr[rows(rc), :], w_in_ref[:, C_U:C_U + SSM_W])
        for j in range(segs_per_chunk):
            for s4 in range(n_slab):
                u_scr[s4, seg_rows(rc, j), :] = u[j * L:(j + 1) * L, s4 * LANES:(s4 + 1) * LANES]

    def permute_rows(i, carry):
        dst = pl.multiple_of(i * N_SEG, N_SEG)
        for s4 in range(n_slab):
            up_scr[s4, pl.ds(dst, N_SEG), :] = u_scr[s4, pl.ds(i, N_SEG, stride=seg_pitch), :]
        return carry

    lax.fori_loop(0, L, permute_rows, 0)

    tile = (N_SEG, CHUNK_STATES)
    seg_in_seq = lax.broadcasted_iota(jnp.int32, tile, 0) & (q - 1)
    c0, c1, c2, c3, c4 = (i * CHUNK_STATES for i in range(5))
    last = (L - 1) * N_SEG

    def cmul(ar, ai, br_, bi_):
        return ar * br_ - ai * bi_, ar * bi_ + ai * br_

    for c in range(N_CHUNK):
        slab, half = divmod(c, 2)
        for rc in range(n_rc):
            big_scr[rows(rc), :] = _dot(up_scr[slab, rows(rc), :].astype(BF16), bm_ref[c])

        def sp_row(i, d):
            return jnp.broadcast_to(sp_ref[c, i:i + 1, d * CHUNK_STATES:(d + 1) * CHUNK_STATES], tile)

        lfr, lfi, lbr, lbi = sp_row(0, 0), sp_row(1, 0), sp_row(0, 1), sp_row(1, 1)

        def scan_step(i, carry):
            hfr, hfi, hbr, hbi = carry
            rf = pl.multiple_of(i * N_SEG, N_SEG)
            rb = pl.multiple_of((L - 1 - i) * N_SEG, N_SEG)
            nfr = lfr * hfr - lfi * hfi + big_scr[pl.ds(rf, N_SEG), c0:c1]
            nfi = lfr * hfi + lfi * hfr + big_scr[pl.ds(rf, N_SEG), c1:c2]
            nbr = lbr * hbr - lbi * hbi + big_scr[pl.ds(rb, N_SEG), c2:c3]
            nbi = lbr * hbi + lbi * hbr + big_scr[pl.ds(rb, N_SEG), c3:c4]
            big_scr[pl.ds(rf, N_SEG), c0:c1] = nfr
            big_scr[pl.ds(rf, N_SEG), c1:c2] = nfi
            big_scr[pl.ds(rb, N_SEG), c2:c3] = nbr
            big_scr[pl.ds(rb, N_SEG), c3:c4] = nbi
            return nfr, nfi, nbr, nbi

        z = jnp.zeros(tile, F32)
        lax.fori_loop(0, L, scan_step, (z, z, z, z), unroll=4)

        efr, efi = big_scr[last:last + N_SEG, c0:c1], big_scr[last:last + N_SEG, c1:c2]
        ebr, ebi = big_scr[0:N_SEG, c2:c3], big_scr[0:N_SEG, c3:c4]
        if latent:
            h0 = [jnp.broadcast_to(r[0, d, c:c + 1, :], tile) for d in range(2) for r in (h0re_ref, h0im_ref)]
        else:
            h0 = [z, z, z, z]
        hfr = jnp.where(seg_in_seq == 0, h0[0], pltpu.roll(efr, 1, 0))
        hfi = jnp.where(seg_in_seq == 0, h0[1], pltpu.roll(efi, 1, 0))
        hbr = jnp.where(seg_in_seq == q - 1, h0[2], pltpu.roll(ebr, N_SEG - 1, 0))
        hbi = jnp.where(seg_in_seq == q - 1, h0[3], pltpu.roll(ebi, N_SEG - 1, 0))
        for j, dist in enumerate((1, 2, 4)):
            if dist >= q:
                break
            pfr, pfi, pbr, pbi = sp_row(2 + 2 * j, 0), sp_row(3 + 2 * j, 0), sp_row(2 + 2 * j, 1), sp_row(3 + 2 * j, 1)
            ar, ai = cmul(pfr, pfi, pltpu.roll(hfr, dist, 0), pltpu.roll(hfi, dist, 0))
            keep = seg_in_seq >= dist
            hfr, hfi = hfr + jnp.where(keep, ar, 0.0), hfi + jnp.where(keep, ai, 0.0)
            ar, ai = cmul(pbr, pbi, pltpu.roll(hbr, N_SEG - dist, 0), pltpu.roll(hbi, N_SEG - dist, 0))
            keep = seg_in_seq <= q - 1 - dist
            hbr, hbi = hbr + jnp.where(keep, ar, 0.0), hbi + jnp.where(keep, ai, 0.0)

        if not latent:
            plr, pli = sp_row(2, 0), sp_row(3, 0)
            ffr, ffi = cmul(plr, pli, hfr, hfi)
            ffr, ffi = ffr + efr, ffi + efi
            plr, pli = sp_row(2, 1), sp_row(3, 1)
            fbr, fbi = cmul(plr, pli, hbr, hbi)
            fbr, fbi = fbr + ebr, fbi + ebi
            cols = slice(c * CHUNK_STATES, (c + 1) * CHUNK_STATES)
            for sq in range(nseq):
                lf = (sq + 1) * q - 1
                lb = sq * q
                st_re_ref[sq, 0:1, cols] = ffr[lf:lf + 1, :]
                st_im_ref[sq, 0:1, cols] = ffi[lf:lf + 1, :]
                st_re_ref[sq, 1:2, cols] = fbr[lb:lb + 1, :]
                st_im_ref[sq, 1:2, cols] = fbi[lb:lb + 1, :]

        def stitch_step(i, carry):
            gfr, gfi, gbr, gbi = carry
            rf = pl.multiple_of(i * N_SEG, N_SEG)
            rb = pl.multiple_of((L - 1 - i) * N_SEG, N_SEG)
            gfr, gfi = cmul(gfr, gfi, lfr, lfi)
            gbr, gbi = cmul(gbr, gbi, lbr, lbi)
            big_scr[pl.ds(rf, N_SEG), c0:c1] += gfr
            big_scr[pl.ds(rf, N_SEG), c1:c2] += gfi
            big_scr[pl.ds(rb, N_SEG), c2:c3] += gbr
            big_scr[pl.ds(rb, N_SEG), c3:c4] += gbi
            return gfr, gfi, gbr, gbi

        lax.fori_loop(0, L, stitch_step, (hfr, hfi, hbr, hbi), unroll=4)

        for rc in range(n_rc):
            yc = _dot(big_scr[rows(rc), :].astype(BF16), cm_ref[c])
            if half == 0:
                yp_scr[slab, rows(rc), :] = yc
            else:
                yp_scr[slab, rows(rc), :] += yc

    def unpermute_rows(j, carry):
        src = j * (N_SEG * SUBLANES)
        for k in range(N_SEG):
            dst = pl.multiple_of(k * L + j * SUBLANES, SUBLANES)
            for s4 in range(n_slab):
                big_scr[pl.ds(dst, SUBLANES), s4 * LANES:(s4 + 1) * LANES] = (
                    yp_scr[s4, pl.ds(src + k, SUBLANES, stride=N_SEG), :])
        return carry

    lax.fori_loop(0, L // SUBLANES, unpermute_rows, 0)

    ssm_d = cvec_ref[3:4, :]
    for rc in range(n_rc):
        hb = h_scr[rows(rc), :]
        u = jnp.concatenate(
            [jnp.concatenate([u_scr[s4, seg_rows(rc, j), :] for j in range(segs_per_chunk)], axis=0)
             for s4 in range(n_slab)], axis=1)
        yb = big_scr[rows(rc), 0:SSM_W] + ssm_d * u
        z2 = _dot(_gelu_tanh(yb).astype(BF16), w_glu_ref[...])
        yb = z2[:, :SSM_W] * _sigmoid(z2[:, SSM_W:])
        sg = _dot(hb, w_in_ref[:, C_UGATE:C_UGATE + SSM_W])
        t = (yb * _silu(sg)).astype(BF16)
        br = _dot(t, w_ssm_out_ref[...])
        g = _sigmoid(_dot(hb, w_in_ref[:, C_MERGE + D_MODEL:C_MERGE + 2 * D_MODEL]))
        out_ref[rows(rc), :] += g * br

    q_g = cvec_ref[4:5, :]
    k_g = cvec_ref[5:6, 0:KV_W]
    lane = lax.broadcasted_iota(jnp.int32, (ROW_CHUNK, LANES), 1)
    first_half = (lane & (2 * ROPE_PAIRS - 1)) < ROPE_PAIRS

    def rope(xs, cos, sin):
        swapped = jnp.where(first_half, pltpu.roll(xs, LANES - ROPE_PAIRS, 1), pltpu.roll(xs, ROPE_PAIRS, 1))
        return xs * cos + swapped * sin

    def head_ms(x2, n):
        hi = x2.astype(BF16)
        lo = (x2 - hi.astype(F32)).astype(BF16)
        ones = ones_ref[0:n, 0:n]
        return (_dot(hi, ones) + _dot(lo, ones)) * (1.0 / HEAD_DIM)

    for rc in range(n_rc):
        zq = _dot(h_scr[rows(rc), :], w_in_ref[:, C_QKV:C_QKV + ATTN_W + 2 * KV_W])
        qf, kf, vf = zq[:, :ATTN_W], zq[:, ATTN_W:ATTN_W + KV_W], zq[:, ATTN_W + KV_W:]
        qn = qf * lax.rsqrt(head_ms(qf * qf, ATTN_W) + EPS) * q_g
        kn = kf * lax.rsqrt(head_ms(kf * kf, KV_W) + EPS) * k_g
        r0 = rc * ROW_CHUNK
        sq = r0 // S
        t0 = r0 - sq * S
        if latent:
            cos = cos_ref[t0:t0 + ROW_CHUNK, :]
            sin = sin_ref[t0:t0 + ROW_CHUNK, :]
            qn = jnp.concatenate(
                [rope(qn[:, i * LANES:(i + 1) * LANES], cos, sin) for i in range(ATTN_W // LANES)], axis=1)
            kn = rope(kn, cos, sin)
        else:
            k_out_ref[rows(rc), :] = kn
            v_out_ref[rows(rc), :] = vf
        big_scr[rows(rc), 0:ATTN_W] = qn * (HEAD_DIM ** -0.5)
        kall_scr[sq, t0:t0 + ROW_CHUNK, :] = kn
        vall_scr[sq, t0:t0 + ROW_CHUNK, :] = vf
    if latent:
        kall_scr[0, S:lk, :] = ck_ref[0]
        vall_scr[0, S:lk, :] = cvv_ref[0]

    def attend(sq, row0):
        for hd in range(N_HEADS):
            kv = hd // GQ
            qh = big_scr[pl.ds(row0, ROW_CHUNK), hd * HEAD_DIM:(hd + 1) * HEAD_DIM].astype(BF16)
            kh = kall_scr[sq, :, kv * HEAD_DIM:(kv + 1) * HEAD_DIM].astype(BF16)
            vh = vall_scr[sq, :, kv * HEAD_DIM:(kv + 1) * HEAD_DIM].astype(BF16)
            s = lax.dot_general(qh, kh, (((1,), (1,)), ((), ())), preferred_element_type=F32)
            e = jnp.exp(s - jnp.max(s, axis=-1, keepdims=True))
            o = _dot(e.astype(BF16), vh) / jnp.sum(e, axis=-1, keepdims=True)
            big_scr[pl.ds(row0, ROW_CHUNK), ATTN_W + hd * HEAD_DIM:ATTN_W + (hd + 1) * HEAD_DIM] = o

    if latent:
        def attend_chunk(qc, carry):
            attend(0, pl.multiple_of(qc * ROW_CHUNK, ROW_CHUNK))
            return carry
        lax.fori_loop(0, n_rc, attend_chunk, 0)
    else:
        for sq in range(nseq):
            for qc in range(S // ROW_CHUNK):
                attend(sq, sq * S + qc * ROW_CHUNK)

    for rc in range(n_rc):
        hb = h_scr[rows(rc), :]
        ag = _dot(hb, w_in_ref[:, C_AGATE:C_AGATE + ATTN_W])
        t = (big_scr[rows(rc), ATTN_W:2 * ATTN_W] * _silu(ag)).astype(BF16)
        br = _dot(t, w_attn_out_ref[...])
        g = _sigmoid(_dot(hb, w_in_ref[:, C_MERGE + 2 * D_MODEL:C_MERGE + 3 * D_MODEL]))
        mixed = out_ref[rows(rc), :] + g * br
        out_ref[rows(rc), :] = x_ref[rows(rc), :] + gate * _dot(mixed.astype(BF16), w_out_ref[...])


def _const_spec(shape):
    nd = len(shape)
    return pl.BlockSpec(shape, lambda i, _nd=nd: (0,) * _nd, pipeline_mode=pl.Buffered(1))


def _layer_call(cfg, x2d, mods, lw, extra):
    R, S, nseq, latent, past = cfg["R"], cfg["S"], cfg["nseq"], cfg["latent"], cfg["past"]
    n_rows = x2d.shape[0]
    n_blocks = n_rows // R
    weights = [lw["ng"], lw["w_in"], lw["cw"], lw["cvec"], lw["w_conv_out"], lw["bm"], lw["cm"],
               lw["sp_lat" if latent else "sp_ctx"], lw["w_glu"], lw["w_ssm_out"], lw["ones"],
               lw["w_attn_out"], lw["w_out"]]
    if latent:
        mods_spec = pl.BlockSpec((1, SUBLANES, D_MODEL), lambda i: (i + 1, 0, 0))
    else:
        mods_spec = pl.BlockSpec((1, SUBLANES, D_MODEL), lambda i: (0, 0, 0))
    row_mode = dict(pipeline_mode=pl.Buffered(1)) if latent else {}
    in_specs = [pl.BlockSpec((R, D_MODEL), lambda i: (i, 0), **row_mode), mods_spec]
    in_specs += [_const_spec(w.shape) for w in weights]
    args = [x2d, mods] + weights
    out_shape = [jax.ShapeDtypeStruct((n_rows, D_MODEL), F32)]
    out_specs = [pl.BlockSpec((R, D_MODEL), lambda i: (i, 0), **row_mode)]
    if latent:
        cos, sin, ck, cv, h0re, h0im = extra
        in_specs += [_const_spec(cos.shape), _const_spec(sin.shape),
                     pl.BlockSpec((1, past, KV_W), lambda i: (i, 0, 0)),
                     pl.BlockSpec((1, past, KV_W), lambda i: (i, 0, 0)),
                     pl.BlockSpec((1, 2, N_CHUNK, CHUNK_STATES), lambda i: (i, 0, 0, 0)),
                     pl.BlockSpec((1, 2, N_CHUNK, CHUNK_STATES), lambda i: (i, 0, 0, 0))]
        args += [cos, sin, ck, cv, h0re, h0im]
    else:
        n_seq_total = n_rows // S
        out_shape += [jax.ShapeDtypeStruct((n_rows, KV_W), F32), jax.ShapeDtypeStruct((n_rows, KV_W), F32),
                      jax.ShapeDtypeStruct((n_seq_total, 2, SSM_G * SSM_P), F32),
                      jax.ShapeDtypeStruct((n_seq_total, 2, SSM_G * SSM_P), F32)]
        out_specs += [pl.BlockSpec((R, KV_W), lambda i: (i, 0)), pl.BlockSpec((R, KV_W), lambda i: (i, 0)),
                      pl.BlockSpec((nseq, 2, SSM_G * SSM_P), lambda i: (i, 0, 0)),
                      pl.BlockSpec((nseq, 2, SSM_G * SSM_P), lambda i: (i, 0, 0))]
    scratch = [
        pltpu.VMEM((R, D_MODEL), BF16),
        pltpu.VMEM((CONV_W // LANES, nseq * (S + 2 * CONV_HALO), LANES), F32),
        pltpu.VMEM((SSM_W // LANES, N_SEG * (cfg["L"] + SEG_PAD), LANES), F32),
        pltpu.VMEM((SSM_W // LANES, R, LANES), F32),
        pltpu.VMEM((SSM_W // LANES, R, LANES), F32),
        pltpu.VMEM((R, SCAN_COLS), F32),
        pltpu.VMEM((nseq, S + past, KV_W), F32),
        pltpu.VMEM((nseq, S + past, KV_W), F32),
    ]
    return pl.pallas_call(
        functools.partial(_layer_kernel, cfg),
        grid=(n_blocks,),
        in_specs=in_specs,
        out_specs=out_specs,
        out_shape=out_shape,
        scratch_shapes=scratch,
        compiler_params=pltpu.CompilerParams(
            dimension_semantics=("arbitrary",),
            vmem_limit_bytes=cfg["vmem_bytes"]),
        name="latent_layer" if latent else "context_layer",
    )(*args)


def _ssm_pack(a_re, a_im, log_dt, b_re, b_im, c_re, c_im, seg_len):
    lam = lax.complex(a_re, a_im)
    dt = jnp.exp(log_dt)[..., None]
    lam_bar = jnp.exp(lam * dt)
    b_bar = ((lam_bar - 1.0) / lam)[..., None] * lax.complex(b_re, b_im)
    eye = jnp.eye(SCAN_GROUPS, dtype=F32)
    bb = jnp.stack([b_bar.real, b_bar.imag], axis=1)
    bb = bb.reshape(2, 2, N_CHUNK, SCAN_GROUPS, SSM_P, SSM_GC)
    tb = jnp.einsum("dacgpk,hg->chkdagp", bb, eye).reshape(N_CHUNK, SCAN_GROUPS * SSM_GC, SCAN_COLS)
    zb = jnp.zeros_like(tb)
    odd = (jnp.arange(N_CHUNK) % 2 == 1)[:, None, None]
    bm = jnp.where(odd, jnp.concatenate([zb, tb], axis=1), jnp.concatenate([tb, zb], axis=1))
    cc = jnp.stack([c_re, -c_im], axis=1)
    cc = cc.reshape(2, 2, N_CHUNK, SCAN_GROUPS, SSM_GC, SSM_P)
    tc = jnp.einsum("dacgkp,hg->cdagphk", cc, eye).reshape(N_CHUNK, SCAN_COLS, SCAN_GROUPS * SSM_GC)
    zc = jnp.zeros_like(tc)
    cm = jnp.where(odd, jnp.concatenate([zc, tc], axis=2), jnp.concatenate([tc, zc], axis=2))

    def chunk_rows(v):
        return v.reshape(2, N_CHUNK, CHUNK_STATES).transpose(1, 0, 2).reshape(N_CHUNK, 2 * CHUNK_STATES)

    def sp_for(seg):
        pows = [lam_bar] + [jnp.exp(lam * dt * float(seg * d)) for d in (1, 2, 4)]
        rows_ = []
        for p in pows:
            rows_ += [chunk_rows(p.real), chunk_rows(p.imag)]
        return jnp.stack(rows_, axis=1)

    return bm.astype(BF16), cm.astype(BF16), [sp_for(s) for s in seg_len]


def _rope_tables(n_tok):
    rows = n_tok // GRID_W
    row = jnp.repeat(jnp.arange(rows, dtype=F32), GRID_W)
    col = jnp.tile(jnp.arange(GRID_W, dtype=F32), rows)
    inv = ROPE_BASE ** (-jnp.arange(ROPE_PAIRS, dtype=F32) / ROPE_PAIRS)
    ar, ac = row[:, None] * inv[None, :], col[:, None] * inv[None, :]
    ang = jnp.concatenate([ar, ar, ac, ac], axis=-1)
    sign = jnp.tile(jnp.concatenate([-jnp.ones(ROPE_PAIRS, F32), jnp.ones(ROPE_PAIRS, F32)]), 2)
    reps = LANES // HEAD_DIM
    return jnp.tile(jnp.cos(ang), (1, reps)), jnp.tile(jnp.sin(ang) * sign, (1, reps))


def kernel(x_prompt, x_sample, cache_k, cache_v, state_ssm_re, state_ssm_im, c, c_ctx, norm_g, w_ada, b_ada, w_in, conv_dw_w, conv_dw_b, conv_ln_g, conv_ln_b, w_conv_out, ssm_a_re, ssm_a_im, ssm_log_dt, ssm_b_re, ssm_b_im, ssm_c_re, ssm_c_im, ssm_d, w_ssm_glu, w_ssm_out, q_norm_g, k_norm_g, w_attn_out, w_out):
    batch, seq, d_model = x_prompt.shape
    dec_batch, dec_seq, _ = x_sample.shape
    depth = w_in.shape[0]
    past = cache_k.shape[2]
    assert d_model == D_MODEL and w_in.shape[2] == IN_COLS and dec_seq % GRID_W == 0

    ctx_cfg = dict(R=2 * seq, S=seq, nseq=2, latent=False, past=0, vmem_bytes=52 * 1024 * 1024)
    lat_cfg = dict(R=dec_seq, S=dec_seq, nseq=1, latent=True, past=past, vmem_bytes=60 * 1024 * 1024)
    for cfg in (ctx_cfg, lat_cfg):
        cfg["L"] = cfg["R"] // N_SEG
        cfg["q"] = cfg["S"] // cfg["L"]
        assert cfg["R"] % ROW_CHUNK == 0 and cfg["S"] % ROW_CHUNK == 0 and cfg["q"] in (1, 2, 4, 8)
        assert cfg["vmem_bytes"] <= VMEM_PHYSICAL_BYTES
        assert ROW_CHUNK % cfg["L"] == 0 and ((cfg["L"] + SEG_PAD) // SUBLANES) % 2 == 1

    cond = jnp.zeros((SUBLANES, d_model), F32).at[0].set(c_ctx).at[1:1 + dec_batch].set(c)
    mods_all = _adaln(cond, w_ada, b_ada)
    cos, sin = _rope_tables(dec_seq)
    head_id = jnp.arange(ATTN_W) // HEAD_DIM
    ones = (head_id[:, None] == head_id[None, :]).astype(BF16)

    yp = x_prompt.reshape(batch * seq, d_model)
    ys = x_sample.reshape(dec_batch * dec_seq, d_model)
    ks_out, vs_out, re_out, im_out = [], [], [], []
    for l in range(depth):
        bm, cm, (sp_ctx, sp_lat) = _ssm_pack(
            ssm_a_re[l], ssm_a_im[l], ssm_log_dt[l], ssm_b_re[l], ssm_b_im[l], ssm_c_re[l], ssm_c_im[l],
            (ctx_cfg["L"], lat_cfg["L"]))
        cvec = jnp.zeros((SUBLANES, CONV_W), F32)
        cvec = cvec.at[0].set(conv_dw_b[l]).at[1].set(conv_ln_g[l]).at[2].set(conv_ln_b[l]).at[3].set(ssm_d[l])
        cvec = cvec.at[4].set(jnp.tile(q_norm_g[l], N_HEADS)).at[5].set(jnp.tile(k_norm_g[l], N_HEADS))
        lw = dict(
            ng=norm_g[l].reshape(1, d_model), w_in=w_in[l].astype(BF16),
            cw=jnp.zeros((CONV_K + 1, CONV_W), F32).at[:CONV_K].set(conv_dw_w[l]), cvec=cvec,
            w_conv_out=w_conv_out[l].astype(BF16), bm=bm, cm=cm, sp_ctx=sp_ctx, sp_lat=sp_lat,
            w_glu=w_ssm_glu[l].astype(BF16), w_ssm_out=w_ssm_out[l].astype(BF16), ones=ones,
            w_attn_out=w_attn_out[l].astype(BF16), w_out=w_out[l].astype(BF16))
        m3 = mods_all[l, :1 + dec_batch].reshape(1 + dec_batch, 3, d_model)
        mods = jnp.zeros((1 + dec_batch, SUBLANES, d_model), F32).at[:, :3].set(m3)

        yp, k_l, v_l, re_l, im_l = _layer_call(ctx_cfg, yp, mods, lw, None)
        ks_out.append(k_l.reshape(batch, seq, N_KV_HEADS, HEAD_DIM))
        vs_out.append(v_l.reshape(batch, seq, N_KV_HEADS, HEAD_DIM))
        re_out.append(re_l.reshape(batch, 2, SSM_G, SSM_P))
        im_out.append(im_l.reshape(batch, 2, SSM_G, SSM_P))

        extra = (cos, sin, cache_k[:, l].reshape(dec_batch, past, KV_W), cache_v[:, l].reshape(dec_batch, past, KV_W),
                 state_ssm_re[:, l].reshape(dec_batch, 2, N_CHUNK, CHUNK_STATES),
                 state_ssm_im[:, l].reshape(dec_batch, 2, N_CHUNK, CHUNK_STATES))
        (ys,) = _layer_call(lat_cfg, ys, mods, lw, extra)

    return (yp.reshape(batch, seq, d_model), ys.reshape(dec_batch, dec_seq, d_model),
            jnp.stack(ks_out, axis=1), jnp.stack(vs_out, axis=1),
            jnp.stack(re_out, axis=1), jnp.stack(im_out, axis=1))
```

```python
import functools
import math

import jax
import jax.numpy as jnp
import numpy as np
from jax import lax
from jax.experimental import pallas as pl
from jax.experimental.pallas import tpu as pltpu

F32 = jnp.float32
BF16 = jnp.bfloat16

D_MODEL = 1024
CONV_W = 512
CONV_K = 31
SSM_W = 512
SSM_GC = 16
SSM_G = 32
SSM_P = 64
N_HEADS = 8
N_KV_HEADS = 2
HEAD_DIM = 64
GQ = N_HEADS // N_KV_HEADS
ATTN_W = N_HEADS * HEAD_DIM
KV_W = N_KV_HEADS * HEAD_DIM
ROPE_PAIRS = HEAD_DIM // 4
ROPE_BASE = 10000.0
GRID_W = 64
EPS = 1e-6

C_CONV = 0
C_CGATE = 1024
C_U = 1536
C_UGATE = 2048
C_QKV = 2560
C_AGATE = 3328
C_MERGE = 3840
IN_COLS = 6912

SUBLANES = 8
LANES = 128
N_SEG = SUBLANES
SEG_PAD = 8
SCAN_GROUPS = 4
N_CHUNK = SSM_G // SCAN_GROUPS
CHUNK_STATES = SCAN_GROUPS * SSM_P
SCAN_COLS = 4 * CHUNK_STATES
CONV_TILE = 64
CONV_HALO = 16
ROW_CHUNK = 256
VMEM_PHYSICAL_BYTES = 64 * 1024 * 1024


def _sigmoid(x):
    return jax.nn.sigmoid(x)


def _silu(x):
    return x * jax.nn.sigmoid(x)


def _gelu_tanh(x):
    return 0.5 * x * (1.0 + jnp.tanh(math.sqrt(2.0 / math.pi) * (x + 0.044715 * (x * x * x))))


def _dot(a, b):
    return jnp.dot(a, b, preferred_element_type=F32)


def _adaln_kernel(cond_ref, w_ref, b_ref, o_ref):
    s = _silu(cond_ref[...])
    o_ref[0] = jnp.dot(s, w_ref[0], preferred_element_type=F32,
                       precision=lax.Precision.HIGHEST) + b_ref[0]


def _adaln(cond, w_ada, b_ada):
    depth, d, n = w_ada.shape
    tn = 512
    return pl.pallas_call(
        _adaln_kernel,
        grid=(depth, n // tn),
        in_specs=[
            pl.BlockSpec((SUBLANES, d), lambda l, j: (0, 0)),
            pl.BlockSpec((1, d, tn), lambda l, j: (l, 0, j)),
            pl.BlockSpec((1, 1, tn), lambda l, j: (l, 0, j)),
        ],
        out_specs=pl.BlockSpec((1, SUBLANES, tn), lambda l, j: (l, 0, j)),
        out_shape=jax.ShapeDtypeStruct((depth, SUBLANES, n), F32),
        name="adaln",
    )(cond, w_ada, b_ada.reshape(depth, 1, n))


def _layer_kernel(cfg, *refs):
    R, S, L, q, nseq, latent, past = (cfg[k] for k in ("R", "S", "L", "q", "nseq", "latent", "past"))
    lk = S + past
    n_rc = R // ROW_CHUNK
    it = iter(refs)
    x_ref, mods_ref, ng_ref, w_in_ref, cw_ref, cvec_ref = (next(it) for _ in range(6))
    w_conv_out_ref, bm_ref, cm_ref, sp_ref, w_glu_ref, w_ssm_out_ref = (next(it) for _ in range(6))
    ones_ref, w_attn_out_ref, w_out_ref = (next(it) for _ in range(3))
    if latent:
        cos_ref, sin_ref, ck_ref, cvv_ref, h0re_ref, h0im_ref = (next(it) for _ in range(6))
    out_ref = next(it)
    if not latent:
        k_out_ref, v_out_ref, st_re_ref, st_im_ref = (next(it) for _ in range(4))
    h_scr, ypad_scr, u_scr, up_scr, yp_scr, big_scr, hs_scr, kall_scr, vall_scr = (next(it) for _ in range(9))
    n_slab = SSM_W // LANES
    seg_pitch = L + SEG_PAD

    ng = ng_ref[...]
    shift = mods_ref[0, 0:1, :]
    scale1 = 1.0 + mods_ref[0, 1:2, :]
    gate = mods_ref[0, 2:3, :]
    pitch = S + 2 * CONV_HALO

    def rows(rc):
        return slice(rc * ROW_CHUNK, (rc + 1) * ROW_CHUNK)

    zero_halo = jnp.zeros((CONV_HALO, LANES), F32)
    for sq in range(nseq):
        for s4 in range(n_slab):
            ypad_scr[s4, sq * pitch:sq * pitch + CONV_HALO, :] = zero_halo
            ypad_scr[s4, sq * pitch + CONV_HALO + S:(sq + 1) * pitch, :] = zero_halo
    for rc in range(n_rc):
        x = x_ref[rows(rc), :]
        ms = jnp.mean(x * x, axis=-1, keepdims=True)
        hn = (x * lax.rsqrt(ms + EPS) * ng) * scale1 + shift
        hb = hn.astype(BF16)
        h_scr[rows(rc), :] = hb
        zc = _dot(hb, w_in_ref[:, C_CONV:C_CONV + 2 * CONV_W])
        y = zc[:, :CONV_W] * _sigmoid(zc[:, CONV_W:])
        r0 = rc * ROW_CHUNK
        sq = r0 // S
        p0 = sq * pitch + CONV_HALO + (r0 - sq * S)
        for s4 in range(n_slab):
            ypad_scr[s4, p0:p0 + ROW_CHUNK, :] = y[:, s4 * LANES:(s4 + 1) * LANES]

    conv_b = cvec_ref[0:1, :]
    ln_g = cvec_ref[1:2, :]
    ln_b = cvec_ref[2:3, :]
    log2_s = int(math.log2(S))

    def conv_tile(t, carry):
        r0 = pl.multiple_of(t * CONV_TILE, CONV_TILE)
        sq = lax.shift_right_logical(t * CONV_TILE, log2_s)
        w0 = r0 + sq * (2 * CONV_HALO)
        for s4 in range(n_slab):
            lanes = slice(s4 * LANES, (s4 + 1) * LANES)
            a = jnp.broadcast_to(conv_b[:, lanes], (CONV_TILE, LANES))
            for k in range(CONV_K):
                a = a + cw_ref[k:k + 1, lanes] * ypad_scr[s4, pl.ds(w0 + 1 + k, CONV_TILE, stride=1), :]
            big_scr[pl.ds(r0, CONV_TILE), lanes] = a
        return carry

    lax.fori_loop(0, R // CONV_TILE, conv_tile, 0)

    for rc in range(n_rc):
        hb = h_scr[rows(rc), :]
        cg = _dot(hb, w_in_ref[:, C_CGATE:C_CGATE + CONV_W])
        acc = big_scr[rows(rc), 0:CONV_W]
        mu = jnp.mean(acc, axis=-1, keepdims=True)
        xc = acc - mu
        var = jnp.mean(xc * xc, axis=-1, keepdims=True)
        yl = xc * lax.rsqrt(var + EPS) * ln_g + ln_b
        t = (_silu(yl) * _silu(cg)).astype(BF16)
        br = _dot(t, w_conv_out_ref[...])
        g = _sigmoid(_dot(hb, w_in_ref[:, C_MERGE:C_MERGE + D_MODEL]))
        out_ref[rows(rc), :] = g * br

    segs_per_chunk = ROW_CHUNK // L

    def seg_rows(rc, j):
        k = rc * segs_per_chunk + j
        return slice(k * seg_pitch, k * seg_pitch + L)

    for rc in range(n_rc):
        u = _dot(h_scr[rows(rc), :], w_in_ref[:, C_U:C_U + SSM_W])
        for j in range(segs_per_chunk):
            for s4 in range(n_slab):
                u_scr[s4, seg_rows(rc, j), :] = u[j * L:(j + 1) * L, s4 * LANES:(s4 + 1) * LANES]

    def permute_rows(i, carry):
        dst = pl.multiple_of(i * N_SEG, N_SEG)
        for s4 in range(n_slab):
            up_scr[s4, pl.ds(dst, N_SEG), :] = u_scr[s4, pl.ds(i, N_SEG, stride=seg_pitch), :]
        return carry

    lax.fori_loop(0, L, permute_rows, 0)

    tile = (N_SEG, CHUNK_STATES)
    seg_in_seq = lax.broadcasted_iota(jnp.int32, tile, 0) & (q - 1)
    c0, c1, c2, c3, c4 = (i * CHUNK_STATES for i in range(5))

    def cmul(ar, ai, br_, bi_):
        return ar * br_ - ai * bi_, ar * bi_ + ai * br_

    for c in range(N_CHUNK):
        slab, half = divmod(c, 2)
        for rc in range(n_rc):
            big_scr[rows(rc), :] = _dot(up_scr[slab, rows(rc), :].astype(BF16), bm_ref[c])

        def sp_row(i, d):
            return jnp.broadcast_to(sp_ref[c, i:i + 1, d * CHUNK_STATES:(d + 1) * CHUNK_STATES], tile)

        lfr, lfi, lbr, lbi = sp_row(0, 0), sp_row(1, 0), sp_row(0, 1), sp_row(1, 1)

        def fwd_step(hr, hi, row):
            row = pl.multiple_of(row, N_SEG)
            return (lfr * hr - lfi * hi + big_scr[pl.ds(row, N_SEG), c0:c1],
                    lfr * hi + lfi * hr + big_scr[pl.ds(row, N_SEG), c1:c2])

        def bwd_step(hr, hi, row):
            row = pl.multiple_of(row, N_SEG)
            return (lbr * hr - lbi * hi + big_scr[pl.ds(row, N_SEG), c2:c3],
                    lbr * hi + lbi * hr + big_scr[pl.ds(row, N_SEG), c3:c4])

        def ends_step(i, carry):
            hfr, hfi, hbr, hbi = carry
            hfr, hfi = fwd_step(hfr, hfi, i * N_SEG)
            hbr, hbi = bwd_step(hbr, hbi, (L - 1 - i) * N_SEG)
            return hfr, hfi, hbr, hbi

        z = jnp.zeros(tile, F32)
        efr, efi, ebr, ebi = lax.fori_loop(0, L, ends_step, (z, z, z, z), unroll=8)

        if latent:
            h0 = [jnp.broadcast_to(r[0, d, c:c + 1, :], tile) for d in range(2) for r in (h0re_ref, h0im_ref)]
        else:
            h0 = [z, z, z, z]
        hfr = jnp.where(seg_in_seq == 0, h0[0], pltpu.roll(efr, 1, 0))
        hfi = jnp.where(seg_in_seq == 0, h0[1], pltpu.roll(efi, 1, 0))
        hbr = jnp.where(seg_in_seq == q - 1, h0[2], pltpu.roll(ebr, N_SEG - 1, 0))
        hbi = jnp.where(seg_in_seq == q - 1, h0[3], pltpu.roll(ebi, N_SEG - 1, 0))
        for j, dist in enumerate((1, 2, 4)):
            if dist >= q:
                break
            pfr, pfi, pbr, pbi = sp_row(2 + 2 * j, 0), sp_row(3 + 2 * j, 0), sp_row(2 + 2 * j, 1), sp_row(3 + 2 * j, 1)
            ar, ai = cmul(pfr, pfi, pltpu.roll(hfr, dist, 0), pltpu.roll(hfi, dist, 0))
            keep = seg_in_seq >= dist
            hfr, hfi = hfr + jnp.where(keep, ar, 0.0), hfi + jnp.where(keep, ai, 0.0)
            ar, ai = cmul(pbr, pbi, pltpu.roll(hbr, N_SEG - dist, 0), pltpu.roll(hbi, N_SEG - dist, 0))
            keep = seg_in_seq <= q - 1 - dist
            hbr, hbi = hbr + jnp.where(keep, ar, 0.0), hbi + jnp.where(keep, ai, 0.0)

        if not latent:
            plr, pli = sp_row(2, 0), sp_row(3, 0)
            ffr, ffi = cmul(plr, pli, hfr, hfi)
            ffr, ffi = ffr + efr, ffi + efi
            plr, pli = sp_row(2, 1), sp_row(3, 1)
            fbr, fbi = cmul(plr, pli, hbr, hbi)
            fbr, fbi = fbr + ebr, fbi + ebi
            cols = slice(c * CHUNK_STATES, (c + 1) * CHUNK_STATES)
            for sq in range(nseq):
                lf = (sq + 1) * q - 1
                lb = sq * q
                st_re_ref[sq, 0:1, cols] = ffr[lf:lf + 1, :]
                st_im_ref[sq, 0:1, cols] = ffi[lf:lf + 1, :]
                st_re_ref[sq, 1:2, cols] = fbr[lb:lb + 1, :]
                st_im_ref[sq, 1:2, cols] = fbi[lb:lb + 1, :]

        def scan_pair(j, carry):
            hfr, hfi, hbr, hbi = carry
            rf = pl.multiple_of(j * (2 * N_SEG), 2 * N_SEG)
            rb = pl.multiple_of((L - 2 - 2 * j) * N_SEG, 2 * N_SEG)
            f1r, f1i = fwd_step(hfr, hfi, rf)
            f2r, f2i = fwd_step(f1r, f1i, rf + N_SEG)
            b1r, b1i = bwd_step(hbr, hbi, rb + N_SEG)
            b2r, b2i = bwd_step(b1r, b1i, rb)
            hs_scr[pl.ds(rf, 2 * N_SEG), c0:c1] = jnp.concatenate([f1r, f2r], axis=0).astype(BF16)
            hs_scr[pl.ds(rf, 2 * N_SEG), c1:c2] = jnp.concatenate([f1i, f2i], axis=0).astype(BF16)
            hs_scr[pl.ds(rb, 2 * N_SEG), c2:c3] = jnp.concatenate([b2r, b1r], axis=0).astype(BF16)
            hs_scr[pl.ds(rb, 2 * N_SEG), c3:c4] = jnp.concatenate([b2i, b1i], axis=0).astype(BF16)
            return f2r, f2i, b2r, b2i

        lax.fori_loop(0, L // 2, scan_pair, (hfr, hfi, hbr, hbi), unroll=2)

        for rc in range(n_rc):
            yc = _dot(hs_scr[rows(rc), :], cm_ref[c])
            if half == 0:
                yp_scr[slab, rows(rc), :] = yc
            else:
                yp_scr[slab, rows(rc), :] += yc

    def unpermute_rows(j, carry):
        src = j * (N_SEG * SUBLANES)
        for k in range(N_SEG):
            dst = pl.multiple_of(k * L + j * SUBLANES, SUBLANES)
            for s4 in range(n_slab):
                big_scr[pl.ds(dst, SUBLANES), s4 * LANES:(s4 + 1) * LANES] = (
                    yp_scr[s4, pl.ds(src + k, SUBLANES, stride=N_SEG), :])
        return carry

    lax.fori_loop(0, L // SUBLANES, unpermute_rows, 0)

    ssm_d = cvec_ref[3:4, :]
    for rc in range(n_rc):
        hb = h_scr[rows(rc), :]
        u = jnp.concatenate(
            [jnp.concatenate([u_scr[s4, seg_rows(rc, j), :] for j in range(segs_per_chunk)], axis=0)
             for s4 in range(n_slab)], axis=1)
        yb = big_scr[rows(rc), 0:SSM_W] + ssm_d * u
        z2 = _dot(_gelu_tanh(yb).astype(BF16), w_glu_ref[...])
        yb = z2[:, :SSM_W] * _sigmoid(z2[:, SSM_W:])
        sg = _dot(hb, w_in_ref[:, C_UGATE:C_UGATE + SSM_W])
        t = (yb * _silu(sg)).astype(BF16)
        br = _dot(t, w_ssm_out_ref[...])
        g = _sigmoid(_dot(hb, w_in_ref[:, C_MERGE + D_MODEL:C_MERGE + 2 * D_MODEL]))
        out_ref[rows(rc), :] += g * br

    q_g = cvec_ref[4:5, :]
    k_g = cvec_ref[5:6, 0:KV_W]
    lane = lax.broadcasted_iota(jnp.int32, (ROW_CHUNK, LANES), 1)
    first_half = (lane & (2 * ROPE_PAIRS - 1)) < ROPE_PAIRS

    def rope(xs, cos, sin):
        swapped = jnp.where(first_half, pltpu.roll(xs, LANES - ROPE_PAIRS, 1), pltpu.roll(xs, ROPE_PAIRS, 1))
        return xs * cos + swapped * sin

    def head_ms(x2, n):
        hi = x2.astype(BF16)
        lo = (x2 - hi.astype(F32)).astype(BF16)
        ones = ones_ref[0:n, 0:n]
        return (_dot(hi, ones) + _dot(lo, ones)) * (1.0 / HEAD_DIM)

    for rc in range(n_rc):
        zq = _dot(h_scr[rows(rc), :], w_in_ref[:, C_QKV:C_QKV + ATTN_W + 2 * KV_W])
        qf, kf, vf = zq[:, :ATTN_W], zq[:, ATTN_W:ATTN_W + KV_W], zq[:, ATTN_W + KV_W:]
        qn = qf * lax.rsqrt(head_ms(qf * qf, ATTN_W) + EPS) * q_g
        kn = kf * lax.rsqrt(head_ms(kf * kf, KV_W) + EPS) * k_g
        r0 = rc * ROW_CHUNK
        sq = r0 // S
        t0 = r0 - sq * S
        if latent:
            cos = cos_ref[t0:t0 + ROW_CHUNK, :]
            sin = sin_ref[t0:t0 + ROW_CHUNK, :]
            qn = jnp.concatenate(
                [rope(qn[:, i * LANES:(i + 1) * LANES], cos, sin) for i in range(ATTN_W // LANES)], axis=1)
            kn = rope(kn, cos, sin)
        else:
            k_out_ref[rows(rc), :] = kn
            v_out_ref[rows(rc), :] = vf
        big_scr[rows(rc), 0:ATTN_W] = qn * (HEAD_DIM ** -0.5)
        kall_scr[sq, t0:t0 + ROW_CHUNK, :] = kn
        vall_scr[sq, t0:t0 + ROW_CHUNK, :] = vf
    if latent:
        kall_scr[0, S:lk, :] = ck_ref[0]
        vall_scr[0, S:lk, :] = cvv_ref[0]

    def attend(sq, row0):
        for hd in range(N_HEADS):
            kv = hd // GQ
            qh = big_scr[pl.ds(row0, ROW_CHUNK), hd * HEAD_DIM:(hd + 1) * HEAD_DIM].astype(BF16)
            kh = kall_scr[sq, :, kv * HEAD_DIM:(kv + 1) * HEAD_DIM].astype(BF16)
            vh = vall_scr[sq, :, kv * HEAD_DIM:(kv + 1) * HEAD_DIM].astype(BF16)
            s = lax.dot_general(qh, kh, (((1,), (1,)), ((), ())), preferred_element_type=F32)
            e = jnp.exp(s - jnp.max(s, axis=-1, keepdims=True))
            o = _dot(e.astype(BF16), vh) / jnp.sum(e, axis=-1, keepdims=True)
            big_scr[pl.ds(row0, ROW_CHUNK), ATTN_W + hd * HEAD_DIM:ATTN_W + (hd + 1) * HEAD_DIM] = o

    if latent:
        def attend_chunk(qc, carry):
            attend(0, pl.multiple_of(qc * ROW_CHUNK, ROW_CHUNK))
            return carry
        lax.fori_loop(0, n_rc, attend_chunk, 0)
    else:
        for sq in range(nseq):
            for qc in range(S // ROW_CHUNK):
                attend(sq, sq * S + qc * ROW_CHUNK)

    for rc in range(n_rc):
        hb = h_scr[rows(rc), :]
        ag = _dot(hb, w_in_ref[:, C_AGATE:C_AGATE + ATTN_W])
        t = (big_scr[rows(rc), ATTN_W:2 * ATTN_W] * _silu(ag)).astype(BF16)
        br = _dot(t, w_attn_out_ref[...])
        g = _sigmoid(_dot(hb, w_in_ref[:, C_MERGE + 2 * D_MODEL:C_MERGE + 3 * D_MODEL]))
        mixed = out_ref[rows(rc), :] + g * br
        out_ref[rows(rc), :] = x_ref[rows(rc), :] + gate * _dot(mixed.astype(BF16), w_out_ref[...])


def _const_spec(shape):
    nd = len(shape)
    return pl.BlockSpec(shape, lambda i, _nd=nd: (0,) * _nd, pipeline_mode=pl.Buffered(1))


def _layer_call(cfg, x2d, mods, lw, extra):
    R, S, nseq, latent, past = cfg["R"], cfg["S"], cfg["nseq"], cfg["latent"], cfg["past"]
    n_rows = x2d.shape[0]
    n_blocks = n_rows // R
    weights = [lw["ng"], lw["w_in"], lw["cw"], lw["cvec"], lw["w_conv_out"], lw["bm"], lw["cm"],
               lw["sp_lat" if latent else "sp_ctx"], lw["w_glu"], lw["w_ssm_out"], lw["ones"],
               lw["w_attn_out"], lw["w_out"]]
    if latent:
        mods_spec = pl.BlockSpec((1, SUBLANES, D_MODEL), lambda i: (i + 1, 0, 0))
    else:
        mods_spec = pl.BlockSpec((1, SUBLANES, D_MODEL), lambda i: (0, 0, 0))
    row_mode = dict(pipeline_mode=pl.Buffered(1)) if latent else {}
    in_specs = [pl.BlockSpec((R, D_MODEL), lambda i: (i, 0), **row_mode), mods_spec]
    in_specs += [_const_spec(w.shape) for w in weights]
    args = [x2d, mods] + weights
    out_shape = [jax.ShapeDtypeStruct((n_rows, D_MODEL), F32)]
    out_specs = [pl.BlockSpec((R, D_MODEL), lambda i: (i, 0), **row_mode)]
    if latent:
        cos, sin, ck, cv, h0re, h0im = extra
        in_specs += [_const_spec(cos.shape), _const_spec(sin.shape),
                     pl.BlockSpec((1, past, KV_W), lambda i: (i, 0, 0)),
                     pl.BlockSpec((1, past, KV_W), lambda i: (i, 0, 0)),
                     pl.BlockSpec((1, 2, N_CHUNK, CHUNK_STATES), lambda i: (i, 0, 0, 0)),
                     pl.BlockSpec((1, 2, N_CHUNK, CHUNK_STATES), lambda i: (i, 0, 0, 0))]
        args += [cos, sin, ck, cv, h0re, h0im]
    else:
        n_seq_total = n_rows // S
        out_shape += [jax.ShapeDtypeStruct((n_rows, KV_W), F32), jax.ShapeDtypeStruct((n_rows, KV_W), F32),
                      jax.ShapeDtypeStruct((n_seq_total, 2, SSM_G * SSM_P), F32),
                      jax.ShapeDtypeStruct((n_seq_total, 2, SSM_G * SSM_P), F32)]
        out_specs += [pl.BlockSpec((R, KV_W), lambda i: (i, 0)), pl.BlockSpec((R, KV_W), lambda i: (i, 0)),
                      pl.BlockSpec((nseq, 2, SSM_G * SSM_P), lambda i: (i, 0, 0)),
                      pl.BlockSpec((nseq, 2, SSM_G * SSM_P), lambda i: (i, 0, 0))]
    scratch = [
        pltpu.VMEM((R, D_MODEL), BF16),
        pltpu.VMEM((CONV_W // LANES, nseq * (S + 2 * CONV_HALO), LANES), F32),
        pltpu.VMEM((SSM_W // LANES, N_SEG * (cfg["L"] + SEG_PAD), LANES), F32),
        pltpu.VMEM((SSM_W // LANES, R, LANES), F32),
        pltpu.VMEM((SSM_W // LANES, R, LANES), F32),
        pltpu.VMEM((R, SCAN_COLS), F32),
        pltpu.VMEM((R, SCAN_COLS), BF16),
        pltpu.VMEM((nseq, S + past, KV_W), F32),
        pltpu.VMEM((nseq, S + past, KV_W), F32),
    ]
    return pl.pallas_call(
        functools.partial(_layer_kernel, cfg),
        grid=(n_blocks,),
        in_specs=in_specs,
        out_specs=out_specs,
        out_shape=out_shape,
        scratch_shapes=scratch,
        compiler_params=pltpu.CompilerParams(
            dimension_semantics=("arbitrary",),
            vmem_limit_bytes=cfg["vmem_bytes"]),
        name="latent_layer" if latent else "context_layer",
    )(*args)


def _ssm_pack(a_re, a_im, log_dt, b_re, b_im, c_re, c_im, seg_len):
    lam = lax.complex(a_re, a_im)
    dt = jnp.exp(log_dt)[..., None]
    lam_bar = jnp.exp(lam * dt)
    b_bar = ((lam_bar - 1.0) / lam)[..., None] * lax.complex(b_re, b_im)
    eye = jnp.eye(SCAN_GROUPS, dtype=F32)
    bb = jnp.stack([b_bar.real, b_bar.imag], axis=1)
    bb = bb.reshape(2, 2, N_CHUNK, SCAN_GROUPS, SSM_P, SSM_GC)
    tb = jnp.einsum("dacgpk,hg->chkdagp", bb, eye).reshape(N_CHUNK, SCAN_GROUPS * SSM_GC, SCAN_COLS)
    zb = jnp.zeros_like(tb)
    odd = (jnp.arange(N_CHUNK) % 2 == 1)[:, None, None]
    bm = jnp.where(odd, jnp.concatenate([zb, tb], axis=1), jnp.concatenate([tb, zb], axis=1))
    cc = jnp.stack([c_re, -c_im], axis=1)
    cc = cc.reshape(2, 2, N_CHUNK, SCAN_GROUPS, SSM_GC, SSM_P)
    tc = jnp.einsum("dacgkp,hg->cdagphk", cc, eye).reshape(N_CHUNK, SCAN_COLS, SCAN_GROUPS * SSM_GC)
    zc = jnp.zeros_like(tc)
    cm = jnp.where(odd, jnp.concatenate([zc, tc], axis=2), jnp.concatenate([tc, zc], axis=2))

    def chunk_rows(v):
        return v.reshape(2, N_CHUNK, CHUNK_STATES).transpose(1, 0, 2).reshape(N_CHUNK, 2 * CHUNK_STATES)

    def sp_for(seg):
        pows = [lam_bar] + [jnp.exp(lam * dt * float(seg * d)) for d in (1, 2, 4)]
        rows_ = []
        for p in pows:
            rows_ += [chunk_rows(p.real), chunk_rows(p.imag)]
        return jnp.stack(rows_, axis=1)

    return bm.astype(BF16), cm.astype(BF16), [sp_for(s) for s in seg_len]


def _rope_tables(n_tok):
    rows = n_tok // GRID_W
    row = jnp.repeat(jnp.arange(rows, dtype=F32), GRID_W)
    col = jnp.tile(jnp.arange(GRID_W, dtype=F32), rows)
    inv = ROPE_BASE ** (-jnp.arange(ROPE_PAIRS, dtype=F32) / ROPE_PAIRS)
    ar, ac = row[:, None] * inv[None, :], col[:, None] * inv[None, :]
    ang = jnp.concatenate([ar, ar, ac, ac], axis=-1)
    sign = jnp.tile(jnp.concatenate([-jnp.ones(ROPE_PAIRS, F32), jnp.ones(ROPE_PAIRS, F32)]), 2)
    reps = LANES // HEAD_DIM
    return jnp.tile(jnp.cos(ang), (1, reps)), jnp.tile(jnp.sin(ang) * sign, (1, reps))


def kernel(x_prompt, x_sample, cache_k, cache_v, state_ssm_re, state_ssm_im, c, c_ctx, norm_g, w_ada, b_ada, w_in, conv_dw_w, conv_dw_b, conv_ln_g, conv_ln_b, w_conv_out, ssm_a_re, ssm_a_im, ssm_log_dt, ssm_b_re, ssm_b_im, ssm_c_re, ssm_c_im, ssm_d, w_ssm_glu, w_ssm_out, q_norm_g, k_norm_g, w_attn_out, w_out):
    batch, seq, d_model = x_prompt.shape
    dec_batch, dec_seq, _ = x_sample.shape
    depth = w_in.shape[0]
    past = cache_k.shape[2]
    assert d_model == D_MODEL and w_in.shape[2] == IN_COLS and dec_seq % GRID_W == 0

    ctx_cfg = dict(R=2 * seq, S=seq, nseq=2, latent=False, past=0, vmem_bytes=52 * 1024 * 1024)
    lat_cfg = dict(R=dec_seq, S=dec_seq, nseq=1, latent=True, past=past, vmem_bytes=60 * 1024 * 1024)
    for cfg in (ctx_cfg, lat_cfg):
        cfg["L"] = cfg["R"] // N_SEG
        cfg["q"] = cfg["S"] // cfg["L"]
        assert cfg["R"] % ROW_CHUNK == 0 and cfg["S"] % ROW_CHUNK == 0 and cfg["q"] in (1, 2, 4, 8)
        assert cfg["vmem_bytes"] <= VMEM_PHYSICAL_BYTES
        assert ROW_CHUNK % cfg["L"] == 0 and ((cfg["L"] + SEG_PAD) // SUBLANES) % 2 == 1

    cond = jnp.zeros((SUBLANES, d_model), F32).at[0].set(c_ctx).at[1:1 + dec_batch].set(c)
    mods_all = _adaln(cond, w_ada, b_ada)
    cos, sin = _rope_tables(dec_seq)
    head_id = jnp.arange(ATTN_W) // HEAD_DIM
    ones = (head_id[:, None] == head_id[None, :]).astype(BF16)

    yp = x_prompt.reshape(batch * seq, d_model)
    ys = x_sample.reshape(dec_batch * dec_seq, d_model)
    ks_out, vs_out, re_out, im_out = [], [], [], []
    for l in range(depth):
        bm, cm, (sp_ctx, sp_lat) = _ssm_pack(
            ssm_a_re[l], ssm_a_im[l], ssm_log_dt[l], ssm_b_re[l], ssm_b_im[l], ssm_c_re[l], ssm_c_im[l],
            (ctx_cfg["L"], lat_cfg["L"]))
        cvec = jnp.zeros((SUBLANES, CONV_W), F32)
        cvec = cvec.at[0].set(conv_dw_b[l]).at[1].set(conv_ln_g[l]).at[2].set(conv_ln_b[l]).at[3].set(ssm_d[l])
        cvec = cvec.at[4].set(jnp.tile(q_norm_g[l], N_HEADS)).at[5].set(jnp.tile(k_norm_g[l], N_HEADS))
        lw = dict(
            ng=norm_g[l].reshape(1, d_model), w_in=w_in[l].astype(BF16),
            cw=jnp.zeros((CONV_K + 1, CONV_W), F32).at[:CONV_K].set(conv_dw_w[l]), cvec=cvec,
            w_conv_out=w_conv_out[l].astype(BF16), bm=bm, cm=cm, sp_ctx=sp_ctx, sp_lat=sp_lat,
            w_glu=w_ssm_glu[l].astype(BF16), w_ssm_out=w_ssm_out[l].astype(BF16), ones=ones,
            w_attn_out=w_attn_out[l].astype(BF16), w_out=w_out[l].astype(BF16))
        m3 = mods_all[l, :1 + dec_batch].reshape(1 + dec_batch, 3, d_model)
        mods = jnp.zeros((1 + dec_batch, SUBLANES, d_model), F32).at[:, :3].set(m3)

        yp, k_l, v_l, re_l, im_l = _layer_call(ctx_cfg, yp, mods, lw, None)
        ks_out.append(k_l.reshape(batch, seq, N_KV_HEADS, HEAD_DIM))
        vs_out.append(v_l.reshape(batch, seq, N_KV_HEADS, HEAD_DIM))
        re_out.append(re_l.reshape(batch, 2, SSM_G, SSM_P))
        im_out.append(im_l.reshape(batch, 2, SSM_G, SSM_P))

        extra = (cos, sin, cache_k[:, l].reshape(dec_batch, past, KV_W), cache_v[:, l].reshape(dec_batch, past, KV_W),
                 state_ssm_re[:, l].reshape(dec_batch, 2, N_CHUNK, CHUNK_STATES),
                 state_ssm_im[:, l].reshape(dec_batch, 2, N_CHUNK, CHUNK_STATES))
        (ys,) = _layer_call(lat_cfg, ys, mods, lw, extra)

    return (yp.reshape(batch, seq, d_model), ys.reshape(dec_batch, dec_seq, d_model),
            jnp.stack(ks_out, axis=1), jnp.stack(vs_out, axis=1),
            jnp.stack(re_out, axis=1), jnp.stack(im_out, axis=1))
```

```python
import functools
import math

import jax
import jax.numpy as jnp
import numpy as np
from jax import lax
from jax.experimental import pallas as pl
from jax.experimental.pallas import tpu as pltpu

F32 = jnp.float32
BF16 = jnp.bfloat16

D_MODEL = 1024
CONV_W = 512
CONV_K = 31
SSM_W = 512
SSM_GC = 16
SSM_G = 32
SSM_P = 64
N_HEADS = 8
N_KV_HEADS = 2
HEAD_DIM = 64
GQ = N_HEADS // N_KV_HEADS
ATTN_W = N_HEADS * HEAD_DIM
KV_W = N_KV_HEADS * HEAD_DIM
ROPE_PAIRS = HEAD_DIM // 4
ROPE_BASE = 10000.0
GRID_W = 64
EPS = 1e-6

C_CONV = 0
C_CGATE = 1024
C_U = 1536
C_UGATE = 2048
C_QKV = 2560
C_AGATE = 3328
C_MERGE = 3840
IN_COLS = 6912

SUBLANES = 8
LANES = 128
N_SEG = SUBLANES
SEG_PAD = 8
SCAN_GROUPS = 4
N_CHUNK = SSM_G // SCAN_GROUPS
CHUNK_STATES = SCAN_GROUPS * SSM_P
SCAN_COLS = 4 * CHUNK_STATES
CONV_TILE = 64
CONV_HALO = 16
ROW_CHUNK = 256
VMEM_PHYSICAL_BYTES = 64 * 1024 * 1024


def _sigmoid(x):
    return jax.nn.sigmoid(x)


def _silu(x):
    return x * jax.nn.sigmoid(x)


def _gelu_tanh(x):
    return 0.5 * x * (1.0 + jnp.tanh(math.sqrt(2.0 / math.pi) * (x + 0.044715 * (x * x * x))))


def _dot(a, b):
    return jnp.dot(a, b, preferred_element_type=F32)


def _adaln_kernel(cond_ref, w_ref, b_ref, o_ref):
    s = _silu(cond_ref[...])
    o_ref[0] = jnp.dot(s, w_ref[0], preferred_element_type=F32,
                       precision=lax.Precision.HIGHEST) + b_ref[0]


def _adaln(cond, w_ada, b_ada):
    depth, d, n = w_ada.shape
    tn = 512
    return pl.pallas_call(
        _adaln_kernel,
        grid=(depth, n // tn),
        in_specs=[
            pl.BlockSpec((SUBLANES, d), lambda l, j: (0, 0)),
            pl.BlockSpec((1, d, tn), lambda l, j: (l, 0, j)),
            pl.BlockSpec((1, 1, tn), lambda l, j: (l, 0, j)),
        ],
        out_specs=pl.BlockSpec((1, SUBLANES, tn), lambda l, j: (l, 0, j)),
        out_shape=jax.ShapeDtypeStruct((depth, SUBLANES, n), F32),
        name="adaln",
    )(cond, w_ada, b_ada.reshape(depth, 1, n))


def _layer_kernel(cfg, *refs):
    R, S, L, q, nseq, latent, past = (cfg[k] for k in ("R", "S", "L", "q", "nseq", "latent", "past"))
    lk = S + past
    n_rc = R // ROW_CHUNK
    it = iter(refs)
    x_ref, mods_ref, ng_ref, w_in_ref, cw_ref, cvec_ref = (next(it) for _ in range(6))
    w_conv_out_ref, bm_ref, cm_ref, sp_ref, w_glu_ref, w_ssm_out_ref = (next(it) for _ in range(6))
    ones_ref, w_attn_out_ref, w_out_ref = (next(it) for _ in range(3))
    if latent:
        cos_ref, sin_ref, ck_ref, cvv_ref, h0re_ref, h0im_ref = (next(it) for _ in range(6))
    out_ref = next(it)
    if not latent:
        k_out_ref, v_out_ref, st_re_ref, st_im_ref = (next(it) for _ in range(4))
    h_scr, ypad_scr, u_scr, up_scr, yp_scr, big_scr, hs_scr, kall_scr, vall_scr = (next(it) for _ in range(9))
    n_slab = SSM_W // LANES
    seg_pitch = L + SEG_PAD

    ng = ng_ref[...]
    shift = mods_ref[0, 0:1, :]
    scale1 = 1.0 + mods_ref[0, 1:2, :]
    gate = mods_ref[0, 2:3, :]
    pitch = S + 2 * CONV_HALO

    def rows(rc):
        return slice(rc * ROW_CHUNK, (rc + 1) * ROW_CHUNK)

    zero_halo = jnp.zeros((CONV_HALO, LANES), F32)
    for sq in range(nseq):
        for s4 in range(n_slab):
            ypad_scr[s4, sq * pitch:sq * pitch + CONV_HALO, :] = zero_halo
            ypad_scr[s4, sq * pitch + CONV_HALO + S:(sq + 1) * pitch, :] = zero_halo
    for rc in range(n_rc):
        x = x_ref[rows(rc), :]
        ms = jnp.mean(x * x, axis=-1, keepdims=True)
        hn = (x * lax.rsqrt(ms + EPS) * ng) * scale1 + shift
        hb = hn.astype(BF16)
        h_scr[rows(rc), :] = hb
        zc = _dot(hb, w_in_ref[:, C_CONV:C_CONV + 2 * CONV_W])
        y = zc[:, :CONV_W] * _sigmoid(zc[:, CONV_W:])
        r0 = rc * ROW_CHUNK
        sq = r0 // S
        p0 = sq * pitch + CONV_HALO + (r0 - sq * S)
        for s4 in range(n_slab):
            ypad_scr[s4, p0:p0 + ROW_CHUNK, :] = y[:, s4 * LANES:(s4 + 1) * LANES]

    conv_b = cvec_ref[0:1, :]
    ln_g = cvec_ref[1:2, :]
    ln_b = cvec_ref[2:3, :]
    log2_s = int(math.log2(S))

    def conv_tile(t, carry):
        r0 = pl.multiple_of(t * CONV_TILE, CONV_TILE)
        sq = lax.shift_right_logical(t * CONV_TILE, log2_s)
        w0 = r0 + sq * (2 * CONV_HALO)
        for s4 in range(n_slab):
            lanes = slice(s4 * LANES, (s4 + 1) * LANES)
            a = jnp.broadcast_to(conv_b[:, lanes], (CONV_TILE, LANES))
            for k in range(CONV_K):
                a = a + cw_ref[k:k + 1, lanes] * ypad_scr[s4, pl.ds(w0 + 1 + k, CONV_TILE, stride=1), :]
            big_scr[pl.ds(r0, CONV_TILE), lanes] = a
        return carry

    lax.fori_loop(0, R // CONV_TILE, conv_tile, 0)

    for rc in range(n_rc):
        hb = h_scr[rows(rc), :]
        cg = _dot(hb, w_in_ref[:, C_CGATE:C_CGATE + CONV_W])
        acc = big_scr[rows(rc), 0:CONV_W]
        mu = jnp.mean(acc, axis=-1, keepdims=True)
        xc = acc - mu
        var = jnp.mean(xc * xc, axis=-1, keepdims=True)
        yl = xc * lax.rsqrt(var + EPS) * ln_g + ln_b
        t = (_silu(yl) * _silu(cg)).astype(BF16)
        br = _dot(t, w_conv_out_ref[...])
        g = _sigmoid(_dot(hb, w_in_ref[:, C_MERGE:C_MERGE + D_MODEL]))
        out_ref[rows(rc), :] = g * br

    segs_per_chunk = ROW_CHUNK // L

    def seg_rows(rc, j):
        k = rc * segs_per_chunk + j
        return slice(k * seg_pitch, k * seg_pitch + L)

    for rc in range(n_rc):
        u = _dot(h_scr[rows(rc), :], w_in_ref[:, C_U:C_U + SSM_W])
        for j in range(segs_per_chunk):
            for s4 in range(n_slab):
                u_scr[s4, seg_rows(rc, j), :] = u[j * L:(j + 1) * L, s4 * LANES:(s4 + 1) * LANES]

    def permute_rows(i, carry):
        dst = pl.multiple_of(i * N_SEG, N_SEG)
        for s4 in range(n_slab):
            up_scr[s4, pl.ds(dst, N_SEG), :] = u_scr[s4, pl.ds(i, N_SEG, stride=seg_pitch), :]
        return carry

    lax.fori_loop(0, L, permute_rows, 0)

    tile = (N_SEG, CHUNK_STATES)
    seg_in_seq = lax.broadcasted_iota(jnp.int32, tile, 0) & (q - 1)
    c0, c1, c2, c3, c4 = (i * CHUNK_STATES for i in range(5))

    def cmul(ar, ai, br_, bi_):
        return ar * br_ - ai * bi_, ar * bi_ + ai * br_

    for c in range(N_CHUNK):
        slab, half = divmod(c, 2)
        for rc in range(n_rc):
            big_scr[rows(rc), :] = _dot(up_scr[slab, rows(rc), :].astype(BF16), bm_ref[c])

        def sp_row(i, d):
            return jnp.broadcast_to(sp_ref[c, i:i + 1, d * CHUNK_STATES:(d + 1) * CHUNK_STATES], tile)

        lfr, lfi, lbr, lbi = sp_row(0, 0), sp_row(1, 0), sp_row(0, 1), sp_row(1, 1)

        def fwd_step(hr, hi, row):
            row = pl.multiple_of(row, N_SEG)
            return (lfr * hr - lfi * hi + big_scr[pl.ds(row, N_SEG), c0:c1],
                    lfr * hi + lfi * hr + big_scr[pl.ds(row, N_SEG), c1:c2])

        def bwd_step(hr, hi, row):
            row = pl.multiple_of(row, N_SEG)
            return (lbr * hr - lbi * hi + big_scr[pl.ds(row, N_SEG), c2:c3],
                    lbr * hi + lbi * hr + big_scr[pl.ds(row, N_SEG), c3:c4])

        def ends_step(i, carry):
            hfr, hfi, hbr, hbi = carry
            hfr, hfi = fwd_step(hfr, hfi, i * N_SEG)
            hbr, hbi = bwd_step(hbr, hbi, (L - 1 - i) * N_SEG)
            return hfr, hfi, hbr, hbi

        z = jnp.zeros(tile, F32)
        efr, efi, ebr, ebi = lax.fori_loop(0, L, ends_step, (z, z, z, z), unroll=8)

        if latent:
            h0 = [jnp.broadcast_to(r[0, d, c:c + 1, :], tile) for d in range(2) for r in (h0re_ref, h0im_ref)]
        else:
            h0 = [z, z, z, z]
        hfr = jnp.where(seg_in_seq == 0, h0[0], pltpu.roll(efr, 1, 0))
        hfi = jnp.where(seg_in_seq == 0, h0[1], pltpu.roll(efi, 1, 0))
        hbr = jnp.where(seg_in_seq == q - 1, h0[2], pltpu.roll(ebr, N_SEG - 1, 0))
        hbi = jnp.where(seg_in_seq == q - 1, h0[3], pltpu.roll(ebi, N_SEG - 1, 0))
        for j, dist in enumerate((1, 2, 4)):
            if dist >= q:
                break
            pfr, pfi, pbr, pbi = sp_row(2 + 2 * j, 0), sp_row(3 + 2 * j, 0), sp_row(2 + 2 * j, 1), sp_row(3 + 2 * j, 1)
            ar, ai = cmul(pfr, pfi, pltpu.roll(hfr, dist, 0), pltpu.roll(hfi, dist, 0))
            keep = seg_in_seq >= dist
            hfr, hfi = hfr + jnp.where(keep, ar, 0.0), hfi + jnp.where(keep, ai, 0.0)
            ar, ai = cmul(pbr, pbi, pltpu.roll(hbr, N_SEG - dist, 0), pltpu.roll(hbi, N_SEG - dist, 0))
            keep = seg_in_seq <= q - 1 - dist
            hbr, hbi = hbr + jnp.where(keep, ar, 0.0), hbi + jnp.where(keep, ai, 0.0)

        if not latent:
            plr, pli = sp_row(2, 0), sp_row(3, 0)
            ffr, ffi = cmul(plr, pli, hfr, hfi)
            ffr, ffi = ffr + efr, ffi + efi
            plr, pli = sp_row(2, 1), sp_row(3, 1)
            fbr, fbi = cmul(plr, pli, hbr, hbi)
            fbr, fbi = fbr + ebr, fbi + ebi
            cols = slice(c * CHUNK_STATES, (c + 1) * CHUNK_STATES)
            for sq in range(nseq):
                lf = (sq + 1) * q - 1
                lb = sq * q
                st_re_ref[sq, 0:1, cols] = ffr[lf:lf + 1, :]
                st_im_ref[sq, 0:1, cols] = ffi[lf:lf + 1, :]
                st_re_ref[sq, 1:2, cols] = fbr[lb:lb + 1, :]
                st_im_ref[sq, 1:2, cols] = fbi[lb:lb + 1, :]

        def scan_pair(j, carry):
            hfr, hfi, hbr, hbi = carry
            rf = pl.multiple_of(j * (2 * N_SEG), 2 * N_SEG)
            rb = pl.multiple_of((L - 2 - 2 * j) * N_SEG, 2 * N_SEG)
            f1r, f1i = fwd_step(hfr, hfi, rf)
            f2r, f2i = fwd_step(f1r, f1i, rf + N_SEG)
            b1r, b1i = bwd_step(hbr, hbi, rb + N_SEG)
            b2r, b2i = bwd_step(b1r, b1i, rb)
            hs_scr[pl.ds(rf, 2 * N_SEG), c0:c1] = jnp.concatenate([f1r, f2r], axis=0).astype(BF16)
            hs_scr[pl.ds(rf, 2 * N_SEG), c1:c2] = jnp.concatenate([f1i, f2i], axis=0).astype(BF16)
            hs_scr[pl.ds(rb, 2 * N_SEG), c2:c3] = jnp.concatenate([b2r, b1r], axis=0).astype(BF16)
            hs_scr[pl.ds(rb, 2 * N_SEG), c3:c4] = jnp.concatenate([b2i, b1i], axis=0).astype(BF16)
            return f2r, f2i, b2r, b2i

        lax.fori_loop(0, L // 2, scan_pair, (hfr, hfi, hbr, hbi), unroll=2)

        for rc in range(n_rc):
            yc = _dot(hs_scr[rows(rc), :], cm_ref[c])
            if half == 0:
                yp_scr[slab, rows(rc), :] = yc
            else:
                yp_scr[slab, rows(rc), :] += yc

    def unpermute_rows(j, carry):
        src = j * (N_SEG * SUBLANES)
        for k in range(N_SEG):
            dst = pl.multiple_of(k * L + j * SUBLANES, SUBLANES)
            for s4 in range(n_slab):
                big_scr[pl.ds(dst, SUBLANES), s4 * LANES:(s4 + 1) * LANES] = (
                    yp_scr[s4, pl.ds(src + k, SUBLANES, stride=N_SEG), :])
        return carry

    lax.fori_loop(0, L // SUBLANES, unpermute_rows, 0)

    ssm_d = cvec_ref[3:4, :]
    for rc in range(n_rc):
        hb = h_scr[rows(rc), :]
        u = jnp.concatenate(
            [jnp.concatenate([u_scr[s4, seg_rows(rc, j), :] for j in range(segs_per_chunk)], axis=0)
             for s4 in range(n_slab)], axis=1)
        yb = big_scr[rows(rc), 0:SSM_W] + ssm_d * u
        z2 = _dot(_gelu_tanh(yb).astype(BF16), w_glu_ref[...])
        yb = z2[:, :SSM_W] * _sigmoid(z2[:, SSM_W:])
        sg = _dot(hb, w_in_ref[:, C_UGATE:C_UGATE + SSM_W])
        t = (yb * _silu(sg)).astype(BF16)
        br = _dot(t, w_ssm_out_ref[...])
        g = _sigmoid(_dot(hb, w_in_ref[:, C_MERGE + D_MODEL:C_MERGE + 2 * D_MODEL]))
        out_ref[rows(rc), :] += g * br

    q_g = cvec_ref[4:5, :]
    k_g = cvec_ref[5:6, 0:KV_W]
    lane = lax.broadcasted_iota(jnp.int32, (ROW_CHUNK, LANES), 1)
    first_half = (lane & (2 * ROPE_PAIRS - 1)) < ROPE_PAIRS

    def rope(xs, cos, sin):
        swapped = jnp.where(first_half, pltpu.roll(xs, LANES - ROPE_PAIRS, 1), pltpu.roll(xs, ROPE_PAIRS, 1))
        return xs * cos + swapped * sin

    def head_ms(x2, n):
        hi = x2.astype(BF16)
        lo = (x2 - hi.astype(F32)).astype(BF16)
        ones = ones_ref[0:n, 0:n]
        return (_dot(hi, ones) + _dot(lo, ones)) * (1.0 / HEAD_DIM)

    for rc in range(n_rc):
        zq = _dot(h_scr[rows(rc), :], w_in_ref[:, C_QKV:C_QKV + ATTN_W + 2 * KV_W])
        qf, kf, vf = zq[:, :ATTN_W], zq[:, ATTN_W:ATTN_W + KV_W], zq[:, ATTN_W + KV_W:]
        qn = qf * lax.rsqrt(head_ms(qf * qf, ATTN_W) + EPS) * q_g
        kn = kf * lax.rsqrt(head_ms(kf * kf, KV_W) + EPS) * k_g
        r0 = rc * ROW_CHUNK
        sq = r0 // S
        t0 = r0 - sq * S
        if latent:
            cos = cos_ref[t0:t0 + ROW_CHUNK, :]
            sin = sin_ref[t0:t0 + ROW_CHUNK, :]
            qn = jnp.concatenate(
                [rope(qn[:, i * LANES:(i + 1) * LANES], cos, sin) for i in range(ATTN_W // LANES)], axis=1)
            kn = rope(kn, cos, sin)
        else:
            k_out_ref[rows(rc), :] = kn
            v_out_ref[rows(rc), :] = vf
        big_scr[rows(rc), 0:ATTN_W] = qn * (HEAD_DIM ** -0.5)
        kall_scr[sq, t0:t0 + ROW_CHUNK, :] = kn
        vall_scr[sq, t0:t0 + ROW_CHUNK, :] = vf
    if latent:
        kall_scr[0, S:lk, :] = ck_ref[0]
        vall_scr[0, S:lk, :] = cvv_ref[0]

    def attend(sq, row0):
        for hd in range(N_HEADS):
            kv = hd // GQ
            qh = big_scr[pl.ds(row0, ROW_CHUNK), hd * HEAD_DIM:(hd + 1) * HEAD_DIM].astype(BF16)
            kh = kall_scr[sq, :, kv * HEAD_DIM:(kv + 1) * HEAD_DIM].astype(BF16)
            vh = vall_scr[sq, :, kv * HEAD_DIM:(kv + 1) * HEAD_DIM].astype(BF16)
            s = lax.dot_general(qh, kh, (((1,), (1,)), ((), ())), preferred_element_type=F32)
            e = jnp.exp(s - jnp.max(s, axis=-1, keepdims=True))
            o = _dot(e.astype(BF16), vh) / jnp.sum(e, axis=-1, keepdims=True)
            big_scr[pl.ds(row0, ROW_CHUNK), ATTN_W + hd * HEAD_DIM:ATTN_W + (hd + 1) * HEAD_DIM] = o

    if latent:
        def attend_chunk(qc, carry):
            attend(0, pl.multiple_of(qc * ROW_CHUNK, ROW_CHUNK))
            return carry
        lax.fori_loop(0, n_rc, attend_chunk, 0)
    else:
        for sq in range(nseq):
            for qc in range(S // ROW_CHUNK):
                attend(sq, sq * S + qc * ROW_CHUNK)

    for rc in range(n_rc):
        hb = h_scr[rows(rc), :]
        ag = _dot(hb, w_in_ref[:, C_AGATE:C_AGATE + ATTN_W])
        t = (big_scr[rows(rc), ATTN_W:2 * ATTN_W] * _silu(ag)).astype(BF16)
        br = _dot(t, w_attn_out_ref[...])
        g = _sigmoid(_dot(hb, w_in_ref[:, C_MERGE + 2 * D_MODEL:C_MERGE + 3 * D_MODEL]))
        mixed = out_ref[rows(rc), :] + g * br
        out_ref[rows(rc), :] = x_ref[rows(rc), :] + gate * _dot(mixed.astype(BF16), w_out_ref[...])


def _const_spec(shape):
    nd = len(shape)
    return pl.BlockSpec(shape, lambda i, _nd=nd: (0,) * _nd, pipeline_mode=pl.Buffered(1))


def _layer_spec(shape, layer):
    nd = len(shape) - 1
    return pl.BlockSpec((None,) + tuple(shape[1:]), lambda i, _nd=nd: (layer,) + (0,) * _nd,
                        pipeline_mode=pl.Buffered(1))


def _layer_call(cfg, layer, x2d, mods, lw, extra):
    R, S, nseq, latent, past = cfg["R"], cfg["S"], cfg["nseq"], cfg["latent"], cfg["past"]
    n_rows = x2d.shape[0]
    n_blocks = n_rows // R
    stacked = [lw["ng"], lw["w_in"], lw["cw"], lw["cvec"], lw["w_conv_out"], lw["bm"], lw["cm"],
               lw["sp_lat" if latent else "sp_ctx"], lw["w_glu"], lw["w_ssm_out"]]
    stacked_tail = [lw["w_attn_out"], lw["w_out"]]
    if latent:
        mods_spec = pl.BlockSpec((None, 1, SUBLANES, D_MODEL), lambda i: (layer, i + 1, 0, 0))
    else:
        mods_spec = pl.BlockSpec((None, 1, SUBLANES, D_MODEL), lambda i: (layer, 0, 0, 0))
    row_mode = dict(pipeline_mode=pl.Buffered(1)) if latent else {}
    in_specs = [pl.BlockSpec((R, D_MODEL), lambda i: (i, 0), **row_mode), mods_spec]
    in_specs += [_layer_spec(w.shape, layer) for w in stacked]
    in_specs += [_const_spec(lw["ones"].shape)]
    in_specs += [_layer_spec(w.shape, layer) for w in stacked_tail]
    args = [x2d, mods] + stacked + [lw["ones"]] + stacked_tail
    out_shape = [jax.ShapeDtypeStruct((n_rows, D_MODEL), F32)]
    out_specs = [pl.BlockSpec((R, D_MODEL), lambda i: (i, 0), **row_mode)]
    if latent:
        cos, sin, ck, cv, h0re, h0im = extra
        in_specs += [_const_spec(cos.shape), _const_spec(sin.shape),
                     pl.BlockSpec((1, None, past, KV_W), lambda i: (i, layer, 0, 0)),
                     pl.BlockSpec((1, None, past, KV_W), lambda i: (i, layer, 0, 0)),
                     pl.BlockSpec((1, None, 2, N_CHUNK, CHUNK_STATES), lambda i: (i, layer, 0, 0, 0)),
                     pl.BlockSpec((1, None, 2, N_CHUNK, CHUNK_STATES), lambda i: (i, layer, 0, 0, 0))]
        args += [cos, sin, ck, cv, h0re, h0im]
    else:
        n_seq_total = n_rows // S
        out_shape += [jax.ShapeDtypeStruct((n_rows, KV_W), F32), jax.ShapeDtypeStruct((n_rows, KV_W), F32),
                      jax.ShapeDtypeStruct((n_seq_total, 2, SSM_G * SSM_P), F32),
                      jax.ShapeDtypeStruct((n_seq_total, 2, SSM_G * SSM_P), F32)]
        out_specs += [pl.BlockSpec((R, KV_W), lambda i: (i, 0)), pl.BlockSpec((R, KV_W), lambda i: (i, 0)),
                      pl.BlockSpec((nseq, 2, SSM_G * SSM_P), lambda i: (i, 0, 0)),
                      pl.BlockSpec((nseq, 2, SSM_G * SSM_P), lambda i: (i, 0, 0))]
    scratch = [
        pltpu.VMEM((R, D_MODEL), BF16),
        pltpu.VMEM((CONV_W // LANES, nseq * (S + 2 * CONV_HALO), LANES), F32),
        pltpu.VMEM((SSM_W // LANES, N_SEG * (cfg["L"] + SEG_PAD), LANES), F32),
        pltpu.VMEM((SSM_W // LANES, R, LANES), F32),
        pltpu.VMEM((SSM_W // LANES, R, LANES), F32),
        pltpu.VMEM((R, SCAN_COLS), F32),
        pltpu.VMEM((R, SCAN_COLS), BF16),
        pltpu.VMEM((nseq, S + past, KV_W), F32),
        pltpu.VMEM((nseq, S + past, KV_W), F32),
    ]
    return pl.pallas_call(
        functools.partial(_layer_kernel, cfg),
        grid=(n_blocks,),
        in_specs=in_specs,
        out_specs=out_specs,
        out_shape=out_shape,
        scratch_shapes=scratch,
        compiler_params=pltpu.CompilerParams(
            dimension_semantics=("arbitrary",),
            vmem_limit_bytes=cfg["vmem_bytes"]),
        name="latent_layer" if latent else "context_layer",
    )(*args)


def _ssm_pack(a_re, a_im, log_dt, b_re, b_im, c_re, c_im, seg_len):
    depth = a_re.shape[0]
    lam = lax.complex(a_re, a_im)
    dt = jnp.exp(log_dt)[..., None]
    lam_bar = jnp.exp(lam * dt)
    b_bar = ((lam_bar - 1.0) / lam)[..., None] * lax.complex(b_re, b_im)
    eye = jnp.eye(SCAN_GROUPS, dtype=F32)
    bb = jnp.stack([b_bar.real, b_bar.imag], axis=2)
    bb = bb.reshape(depth, 2, 2, N_CHUNK, SCAN_GROUPS, SSM_P, SSM_GC)
    tb = jnp.einsum("ldacgpk,hg->lchkdagp", bb, eye).reshape(depth, N_CHUNK, SCAN_GROUPS * SSM_GC, SCAN_COLS)
    zb = jnp.zeros_like(tb)
    odd = (jnp.arange(N_CHUNK) % 2 == 1)[None, :, None, None]
    bm = jnp.where(odd, jnp.concatenate([zb, tb], axis=2), jnp.concatenate([tb, zb], axis=2))
    cc = jnp.stack([c_re, -c_im], axis=2)
    cc = cc.reshape(depth, 2, 2, N_CHUNK, SCAN_GROUPS, SSM_GC, SSM_P)
    tc = jnp.einsum("ldacgkp,hg->lcdagphk", cc, eye).reshape(depth, N_CHUNK, SCAN_COLS, SCAN_GROUPS * SSM_GC)
    zc = jnp.zeros_like(tc)
    cm = jnp.where(odd, jnp.concatenate([zc, tc], axis=3), jnp.concatenate([tc, zc], axis=3))

    def chunk_rows(v):
        v = v.reshape(depth, 2, N_CHUNK, CHUNK_STATES).transpose(0, 2, 1, 3)
        return v.reshape(depth, N_CHUNK, 2 * CHUNK_STATES)

    def sp_for(seg):
        pows = [lam_bar] + [jnp.exp(lam * dt * float(seg * d)) for d in (1, 2, 4)]
        rows_ = []
        for p in pows:
            rows_ += [chunk_rows(p.real), chunk_rows(p.imag)]
        return jnp.stack(rows_, axis=2)

    return bm.astype(BF16), cm.astype(BF16), [sp_for(s) for s in seg_len]


def _rope_tables(n_tok):
    rows = n_tok // GRID_W
    row = jnp.repeat(jnp.arange(rows, dtype=F32), GRID_W)
    col = jnp.tile(jnp.arange(GRID_W, dtype=F32), rows)
    inv = ROPE_BASE ** (-jnp.arange(ROPE_PAIRS, dtype=F32) / ROPE_PAIRS)
    ar, ac = row[:, None] * inv[None, :], col[:, None] * inv[None, :]
    ang = jnp.concatenate([ar, ar, ac, ac], axis=-1)
    sign = jnp.tile(jnp.concatenate([-jnp.ones(ROPE_PAIRS, F32), jnp.ones(ROPE_PAIRS, F32)]), 2)
    reps = LANES // HEAD_DIM
    return jnp.tile(jnp.cos(ang), (1, reps)), jnp.tile(jnp.sin(ang) * sign, (1, reps))


def kernel(x_prompt, x_sample, cache_k, cache_v, state_ssm_re, state_ssm_im, c, c_ctx, norm_g, w_ada, b_ada, w_in, conv_dw_w, conv_dw_b, conv_ln_g, conv_ln_b, w_conv_out, ssm_a_re, ssm_a_im, ssm_log_dt, ssm_b_re, ssm_b_im, ssm_c_re, ssm_c_im, ssm_d, w_ssm_glu, w_ssm_out, q_norm_g, k_norm_g, w_attn_out, w_out):
    batch, seq, d_model = x_prompt.shape
    dec_batch, dec_seq, _ = x_sample.shape
    depth = w_in.shape[0]
    past = cache_k.shape[2]
    assert d_model == D_MODEL and w_in.shape[2] == IN_COLS and dec_seq % GRID_W == 0

    ctx_cfg = dict(R=2 * seq, S=seq, nseq=2, latent=False, past=0, vmem_bytes=52 * 1024 * 1024)
    lat_cfg = dict(R=dec_seq, S=dec_seq, nseq=1, latent=True, past=past, vmem_bytes=60 * 1024 * 1024)
    for cfg in (ctx_cfg, lat_cfg):
        cfg["L"] = cfg["R"] // N_SEG
        cfg["q"] = cfg["S"] // cfg["L"]
        assert cfg["R"] % ROW_CHUNK == 0 and cfg["S"] % ROW_CHUNK == 0 and cfg["q"] in (1, 2, 4, 8)
        assert cfg["vmem_bytes"] <= VMEM_PHYSICAL_BYTES
        assert ROW_CHUNK % cfg["L"] == 0 and ((cfg["L"] + SEG_PAD) // SUBLANES) % 2 == 1

    cond = jnp.zeros((SUBLANES, d_model), F32).at[0].set(c_ctx).at[1:1 + dec_batch].set(c)
    mods_all = _adaln(cond, w_ada, b_ada)
    cos, sin = _rope_tables(dec_seq)
    head_id = jnp.arange(ATTN_W) // HEAD_DIM
    ones = (head_id[:, None] == head_id[None, :]).astype(BF16)

    bm, cm, (sp_ctx, sp_lat) = _ssm_pack(ssm_a_re, ssm_a_im, ssm_log_dt, ssm_b_re, ssm_b_im, ssm_c_re, ssm_c_im,
                                         (ctx_cfg["L"], lat_cfg["L"]))
    zrow = jnp.zeros((depth, CONV_W), F32)
    cvec = jnp.stack([conv_dw_b, conv_ln_g, conv_ln_b, ssm_d, jnp.tile(q_norm_g, (1, N_HEADS)),
                      jnp.tile(k_norm_g, (1, N_HEADS)), zrow, zrow], axis=1)
    lw = dict(
        ng=norm_g.reshape(depth, 1, d_model), w_in=w_in.astype(BF16),
        cw=jnp.concatenate([conv_dw_w, zrow[:, None, :]], axis=1), cvec=cvec,
        w_conv_out=w_conv_out.astype(BF16), bm=bm, cm=cm, sp_ctx=sp_ctx, sp_lat=sp_lat,
        w_glu=w_ssm_glu.astype(BF16), w_ssm_out=w_ssm_out.astype(BF16), ones=ones,
        w_attn_out=w_attn_out.astype(BF16), w_out=w_out.astype(BF16))
    m3 = mods_all[:, :1 + dec_batch].reshape(depth, 1 + dec_batch, 3, d_model)
    mods = jnp.concatenate(
        [m3, jnp.zeros((depth, 1 + dec_batch, SUBLANES - 3, d_model), F32)], axis=2)
    ck_all = cache_k.reshape(dec_batch, depth, past, KV_W)
    cv_all = cache_v.reshape(dec_batch, depth, past, KV_W)
    h0re = state_ssm_re.reshape(dec_batch, depth, 2, N_CHUNK, CHUNK_STATES)
    h0im = state_ssm_im.reshape(dec_batch, depth, 2, N_CHUNK, CHUNK_STATES)

    yp = x_prompt.reshape(batch * seq, d_model)
    ys = x_sample.reshape(dec_batch * dec_seq, d_model)
    ks_out, vs_out, re_out, im_out = [], [], [], []
    for l in range(depth):
        yp, k_l, v_l, re_l, im_l = _layer_call(ctx_cfg, l, yp, mods, lw, None)
        ks_out.append(k_l.reshape(batch, seq, N_KV_HEADS, HEAD_DIM))
        vs_out.append(v_l.reshape(batch, seq, N_KV_HEADS, HEAD_DIM))
        re_out.append(re_l.reshape(batch, 2, SSM_G, SSM_P))
        im_out.append(im_l.reshape(batch, 2, SSM_G, SSM_P))
        (ys,) = _layer_call(lat_cfg, l, ys, mods, lw, (cos, sin, ck_all, cv_all, h0re, h0im))

    return (yp.reshape(batch, seq, d_model), ys.reshape(dec_batch, dec_seq, d_model),
            jnp.stack(ks_out, axis=1), jnp.stack(vs_out, axis=1),
            jnp.stack(re_out, axis=1), jnp.stack(im_out, axis=1))
```

```python
import functools
import math

import jax
import jax.numpy as jnp
import numpy as np
from jax import lax
from jax.experimental import pallas as pl
from jax.experimental.pallas import tpu as pltpu

F32 = jnp.float32
BF16 = jnp.bfloat16

D_MODEL = 1024
CONV_W = 512
CONV_K = 31
SSM_W = 512
SSM_GC = 16
SSM_G = 32
SSM_P = 64
N_HEADS = 8
N_KV_HEADS = 2
HEAD_DIM = 64
GQ = N_HEADS // N_KV_HEADS
ATTN_W = N_HEADS * HEAD_DIM
KV_W = N_KV_HEADS * HEAD_DIM
ROPE_PAIRS = HEAD_DIM // 4
ROPE_BASE = 10000.0
GRID_W = 64
EPS = 1e-6

C_CONV = 0
C_CGATE = 1024
C_U = 1536
C_UGATE = 2048
C_QKV = 2560
C_AGATE = 3328
C_MERGE = 3840
IN_COLS = 6912

SUBLANES = 8
LANES = 128
N_SEG = SUBLANES
SEG_PAD = 8
SCAN_GROUPS = 4
N_CHUNK = SSM_G // SCAN_GROUPS
CHUNK_STATES = SCAN_GROUPS * SSM_P
SCAN_COLS = 4 * CHUNK_STATES
CONV_TILE = 64
CONV_HALO = 16
ROW_CHUNK = 256
VMEM_PHYSICAL_BYTES = 64 * 1024 * 1024


def _sigmoid(x):
    return jax.nn.sigmoid(x)


def _silu(x):
    return x * jax.nn.sigmoid(x)


def _gelu_tanh(x):
    return 0.5 * x * (1.0 + jnp.tanh(math.sqrt(2.0 / math.pi) * (x + 0.044715 * (x * x * x))))


def _dot(a, b):
    return jnp.dot(a, b, preferred_element_type=F32)


def _aligned(x, m):
    return x if isinstance(x, int) else pl.multiple_of(x, m)


def _for(n, body, carry, *, static, unroll=1):
    if static:
        for i in range(n):
            carry = body(i, carry)
        return carry
    return lax.fori_loop(0, n, body, carry, unroll=unroll)


def _adaln_kernel(cond_ref, w_ref, b_ref, o_ref):
    s = _silu(cond_ref[...])
    o_ref[0] = jnp.dot(s, w_ref[0], preferred_element_type=F32,
                       precision=lax.Precision.HIGHEST) + b_ref[0]


def _adaln(cond, w_ada, b_ada):
    depth, d, n = w_ada.shape
    tn = 512
    return pl.pallas_call(
        _adaln_kernel,
        grid=(depth, n // tn),
        in_specs=[
            pl.BlockSpec((SUBLANES, d), lambda l, j: (0, 0)),
            pl.BlockSpec((1, d, tn), lambda l, j: (l, 0, j)),
            pl.BlockSpec((1, 1, tn), lambda l, j: (l, 0, j)),
        ],
        out_specs=pl.BlockSpec((1, SUBLANES, tn), lambda l, j: (l, 0, j)),
        out_shape=jax.ShapeDtypeStruct((depth, SUBLANES, n), F32),
        name="adaln",
    )(cond, w_ada, b_ada.reshape(depth, 1, n))


def _layer_kernel(cfg, *refs):
    R, S, L, q, nseq, latent, past = (cfg[k] for k in ("R", "S", "L", "q", "nseq", "latent", "past"))
    lk = S + past
    n_rc = R // ROW_CHUNK
    it = iter(refs)
    x_ref, mods_ref, ng_ref, w_in_ref, cw_ref, cvec_ref = (next(it) for _ in range(6))
    w_conv_out_ref, bm_ref, cm_ref, sp_ref, w_glu_ref, w_ssm_out_ref = (next(it) for _ in range(6))
    ones_ref, w_attn_out_ref, w_out_ref = (next(it) for _ in range(3))
    if latent:
        cos_ref, sin_ref, ck_ref, cvv_ref, h0re_ref, h0im_ref = (next(it) for _ in range(6))
    out_ref = next(it)
    if not latent:
        k_out_ref, v_out_ref, st_re_ref, st_im_ref = (next(it) for _ in range(4))
    h_scr, ypad_scr, u_scr, up_scr, yp_scr, bigs_scr, hs_scr, kall_scr, vall_scr = (next(it) for _ in range(9))
    static_loops = cfg["static_loops"]
    bigs = [bigs_scr.at[i] for i in range(bigs_scr.shape[0])]
    big_scr = bigs[0]
    n_slab = SSM_W // LANES
    seg_pitch = L + SEG_PAD

    ng = ng_ref[...]
    shift = mods_ref[0, 0:1, :]
    scale1 = 1.0 + mods_ref[0, 1:2, :]
    gate = mods_ref[0, 2:3, :]
    pitch = S + 2 * CONV_HALO

    def rows(rc):
        return slice(rc * ROW_CHUNK, (rc + 1) * ROW_CHUNK)

    zero_halo = jnp.zeros((CONV_HALO, LANES), F32)
    for sq in range(nseq):
        for s4 in range(n_slab):
            ypad_scr[s4, sq * pitch:sq * pitch + CONV_HALO, :] = zero_halo
            ypad_scr[s4, sq * pitch + CONV_HALO + S:(sq + 1) * pitch, :] = zero_halo
    for rc in range(n_rc):
        x = x_ref[rows(rc), :]
        ms = jnp.mean(x * x, axis=-1, keepdims=True)
        hn = (x * lax.rsqrt(ms + EPS) * ng) * scale1 + shift
        hb = hn.astype(BF16)
        h_scr[rows(rc), :] = hb
        zc = _dot(hb, w_in_ref[:, C_CONV:C_CONV + 2 * CONV_W])
        y = zc[:, :CONV_W] * _sigmoid(zc[:, CONV_W:])
        r0 = rc * ROW_CHUNK
        sq = r0 // S
        p0 = sq * pitch + CONV_HALO + (r0 - sq * S)
        for s4 in range(n_slab):
            ypad_scr[s4, p0:p0 + ROW_CHUNK, :] = y[:, s4 * LANES:(s4 + 1) * LANES]

    conv_b = cvec_ref[0:1, :]
    ln_g = cvec_ref[1:2, :]
    ln_b = cvec_ref[2:3, :]
    log2_s = int(math.log2(S))

    def conv_tile(t, carry):
        r0 = _aligned(t * CONV_TILE, CONV_TILE)
        sq = (t * CONV_TILE) >> log2_s
        w0 = r0 + sq * (2 * CONV_HALO)
        for s4 in range(n_slab):
            lanes = slice(s4 * LANES, (s4 + 1) * LANES)
            a = jnp.broadcast_to(conv_b[:, lanes], (CONV_TILE, LANES))
            for k in range(CONV_K):
                a = a + cw_ref[k:k + 1, lanes] * ypad_scr[s4, pl.ds(w0 + 1 + k, CONV_TILE, stride=1), :]
            big_scr[pl.ds(r0, CONV_TILE), lanes] = a
        return carry

    _for(R // CONV_TILE, conv_tile, 0, static=static_loops)

    for rc in range(n_rc):
        hb = h_scr[rows(rc), :]
        cg = _dot(hb, w_in_ref[:, C_CGATE:C_CGATE + CONV_W])
        acc = big_scr[rows(rc), 0:CONV_W]
        mu = jnp.mean(acc, axis=-1, keepdims=True)
        xc = acc - mu
        var = jnp.mean(xc * xc, axis=-1, keepdims=True)
        yl = xc * lax.rsqrt(var + EPS) * ln_g + ln_b
        t = (_silu(yl) * _silu(cg)).astype(BF16)
        br = _dot(t, w_conv_out_ref[...])
        g = _sigmoid(_dot(hb, w_in_ref[:, C_MERGE:C_MERGE + D_MODEL]))
        out_ref[rows(rc), :] = g * br

    segs_per_chunk = ROW_CHUNK // L

    def seg_rows(rc, j):
        k = rc * segs_per_chunk + j
        return slice(k * seg_pitch, k * seg_pitch + L)

    for rc in range(n_rc):
        u = _dot(h_scr[rows(rc), :], w_in_ref[:, C_U:C_U + SSM_W])
        for j in range(segs_per_chunk):
            for s4 in range(n_slab):
                u_scr[s4, seg_rows(rc, j), :] = u[j * L:(j + 1) * L, s4 * LANES:(s4 + 1) * LANES]

    def permute_rows(i, carry):
        dst = pl.multiple_of(i * N_SEG, N_SEG)
        for s4 in range(n_slab):
            up_scr[s4, pl.ds(dst, N_SEG), :] = u_scr[s4, pl.ds(i, N_SEG, stride=seg_pitch), :]
        return carry

    lax.fori_loop(0, L, permute_rows, 0)

    tile = (N_SEG, CHUNK_STATES)
    seg_in_seq = lax.broadcasted_iota(jnp.int32, tile, 0) & (q - 1)
    c0, c1, c2, c3, c4 = (i * CHUNK_STATES for i in range(5))

    def cmul(ar, ai, br_, bi_):
        return ar * br_ - ai * bi_, ar * bi_ + ai * br_

    def input_map(c):
        for rc in range(n_rc):
            bigs[c % len(bigs)][rows(rc), :] = _dot(up_scr[c // 2, rows(rc), :].astype(BF16), bm_ref[c])

    lookahead = len(bigs) - 1
    for c in range(lookahead):
        input_map(c)
    for c in range(N_CHUNK):
        slab, half = divmod(c, 2)
        bu_scr = bigs[c % len(bigs)]
        if c + lookahead < N_CHUNK:
            input_map(c + lookahead)

        def sp_row(i, d):
            return jnp.broadcast_to(sp_ref[c, i:i + 1, d * CHUNK_STATES:(d + 1) * CHUNK_STATES], tile)

        lfr, lfi, lbr, lbi = sp_row(0, 0), sp_row(1, 0), sp_row(0, 1), sp_row(1, 1)

        def fwd_step(hr, hi, row):
            row = _aligned(row, N_SEG)
            return (lfr * hr - lfi * hi + bu_scr[pl.ds(row, N_SEG), c0:c1],
                    lfr * hi + lfi * hr + bu_scr[pl.ds(row, N_SEG), c1:c2])

        def bwd_step(hr, hi, row):
            row = _aligned(row, N_SEG)
            return (lbr * hr - lbi * hi + bu_scr[pl.ds(row, N_SEG), c2:c3],
                    lbr * hi + lbi * hr + bu_scr[pl.ds(row, N_SEG), c3:c4])

        def ends_step(i, carry):
            hfr, hfi, hbr, hbi = carry
            hfr, hfi = fwd_step(hfr, hfi, i * N_SEG)
            hbr, hbi = bwd_step(hbr, hbi, (L - 1 - i) * N_SEG)
            return hfr, hfi, hbr, hbi

        z = jnp.zeros(tile, F32)
        efr, efi, ebr, ebi = _for(L, ends_step, (z, z, z, z), static=static_loops, unroll=8)

        if latent:
            h0 = [jnp.broadcast_to(r[0, d, c:c + 1, :], tile) for d in range(2) for r in (h0re_ref, h0im_ref)]
        else:
            h0 = [z, z, z, z]
        hfr = jnp.where(seg_in_seq == 0, h0[0], pltpu.roll(efr, 1, 0))
        hfi = jnp.where(seg_in_seq == 0, h0[1], pltpu.roll(efi, 1, 0))
        hbr = jnp.where(seg_in_seq == q - 1, h0[2], pltpu.roll(ebr, N_SEG - 1, 0))
        hbi = jnp.where(seg_in_seq == q - 1, h0[3], pltpu.roll(ebi, N_SEG - 1, 0))
        for j, dist in enumerate((1, 2, 4)):
            if dist >= q:
                break
            pfr, pfi, pbr, pbi = sp_row(2 + 2 * j, 0), sp_row(3 + 2 * j, 0), sp_row(2 + 2 * j, 1), sp_row(3 + 2 * j, 1)
            ar, ai = cmul(pfr, pfi, pltpu.roll(hfr, dist, 0), pltpu.roll(hfi, dist, 0))
            keep = seg_in_seq >= dist
            hfr, hfi = hfr + jnp.where(keep, ar, 0.0), hfi + jnp.where(keep, ai, 0.0)
            ar, ai = cmul(pbr, pbi, pltpu.roll(hbr, N_SEG - dist, 0), pltpu.roll(hbi, N_SEG - dist, 0))
            keep = seg_in_seq <= q - 1 - dist
            hbr, hbi = hbr + jnp.where(keep, ar, 0.0), hbi + jnp.where(keep, ai, 0.0)

        if not latent:
            plr, pli = sp_row(2, 0), sp_row(3, 0)
            ffr, ffi = cmul(plr, pli, hfr, hfi)
            ffr, ffi = ffr + efr, ffi + efi
            plr, pli = sp_row(2, 1), sp_row(3, 1)
            fbr, fbi = cmul(plr, pli, hbr, hbi)
            fbr, fbi = fbr + ebr, fbi + ebi
            cols = slice(c * CHUNK_STATES, (c + 1) * CHUNK_STATES)
            for sq in range(nseq):
                lf = (sq + 1) * q - 1
                lb = sq * q
                st_re_ref[sq, 0:1, cols] = ffr[lf:lf + 1, :]
                st_im_ref[sq, 0:1, cols] = ffi[lf:lf + 1, :]
                st_re_ref[sq, 1:2, cols] = fbr[lb:lb + 1, :]
                st_im_ref[sq, 1:2, cols] = fbi[lb:lb + 1, :]

        def scan_pair(j, carry):
            hfr, hfi, hbr, hbi = carry
            rf = _aligned(j * (2 * N_SEG), 2 * N_SEG)
            rb = _aligned((L - 2 - 2 * j) * N_SEG, 2 * N_SEG)
            f1r, f1i = fwd_step(hfr, hfi, rf)
            f2r, f2i = fwd_step(f1r, f1i, rf + N_SEG)
            b1r, b1i = bwd_step(hbr, hbi, rb + N_SEG)
            b2r, b2i = bwd_step(b1r, b1i, rb)
            hs_scr[pl.ds(rf, 2 * N_SEG), c0:c1] = jnp.concatenate([f1r, f2r], axis=0).astype(BF16)
            hs_scr[pl.ds(rf, 2 * N_SEG), c1:c2] = jnp.concatenate([f1i, f2i], axis=0).astype(BF16)
            hs_scr[pl.ds(rb, 2 * N_SEG), c2:c3] = jnp.concatenate([b2r, b1r], axis=0).astype(BF16)
            hs_scr[pl.ds(rb, 2 * N_SEG), c3:c4] = jnp.concatenate([b2i, b1i], axis=0).astype(BF16)
            return f2r, f2i, b2r, b2i

        _for(L // 2, scan_pair, (hfr, hfi, hbr, hbi), static=static_loops, unroll=2)

        for rc in range(n_rc):
            yc = _dot(hs_scr[rows(rc), :], cm_ref[c])
            if half == 0:
                yp_scr[slab, rows(rc), :] = yc
            else:
                yp_scr[slab, rows(rc), :] += yc

    def unpermute_rows(j, carry):
        src = j * (N_SEG * SUBLANES)
        for k in range(N_SEG):
            dst = pl.multiple_of(k * L + j * SUBLANES, SUBLANES)
            for s4 in range(n_slab):
                big_scr[pl.ds(dst, SUBLANES), s4 * LANES:(s4 + 1) * LANES] = (
                    yp_scr[s4, pl.ds(src + k, SUBLANES, stride=N_SEG), :])
        return carry

    lax.fori_loop(0, L // SUBLANES, unpermute_rows, 0)

    ssm_d = cvec_ref[3:4, :]
    for rc in range(n_rc):
        hb = h_scr[rows(rc), :]
        u = jnp.concatenate(
            [jnp.concatenate([u_scr[s4, seg_rows(rc, j), :] for j in range(segs_per_chunk)], axis=0)
             for s4 in range(n_slab)], axis=1)
        yb = big_scr[rows(rc), 0:SSM_W] + ssm_d * u
        z2 = _dot(_gelu_tanh(yb).astype(BF16), w_glu_ref[...])
        yb = z2[:, :SSM_W] * _sigmoid(z2[:, SSM_W:])
        sg = _dot(hb, w_in_ref[:, C_UGATE:C_UGATE + SSM_W])
        t = (yb * _silu(sg)).astype(BF16)
        br = _dot(t, w_ssm_out_ref[...])
        g = _sigmoid(_dot(hb, w_in_ref[:, C_MERGE + D_MODEL:C_MERGE + 2 * D_MODEL]))
        out_ref[rows(rc), :] += g * br

    q_g = cvec_ref[4:5, :]
    k_g = cvec_ref[5:6, 0:KV_W]
    lane = lax.broadcasted_iota(jnp.int32, (ROW_CHUNK, LANES), 1)
    first_half = (lane & (2 * ROPE_PAIRS - 1)) < ROPE_PAIRS

    def rope(xs, cos, sin):
        swapped = jnp.where(first_half, pltpu.roll(xs, LANES - ROPE_PAIRS, 1), pltpu.roll(xs, ROPE_PAIRS, 1))
        return xs * cos + swapped * sin

    def head_ms(x2, n):
        hi = x2.astype(BF16)
        lo = (x2 - hi.astype(F32)).astype(BF16)
        ones = ones_ref[0:n, 0:n]
        return (_dot(hi, ones) + _dot(lo, ones)) * (1.0 / HEAD_DIM)

    for rc in range(n_rc):
        zq = _dot(h_scr[rows(rc), :], w_in_ref[:, C_QKV:C_QKV + ATTN_W + 2 * KV_W])
        qf, kf, vf = zq[:, :ATTN_W], zq[:, ATTN_W:ATTN_W + KV_W], zq[:, ATTN_W + KV_W:]
        qn = qf * lax.rsqrt(head_ms(qf * qf, ATTN_W) + EPS) * q_g
        kn = kf * lax.rsqrt(head_ms(kf * kf, KV_W) + EPS) * k_g
        r0 = rc * ROW_CHUNK
        sq = r0 // S
        t0 = r0 - sq * S
        if latent:
            cos = cos_ref[t0:t0 + ROW_CHUNK, :]
            sin = sin_ref[t0:t0 + ROW_CHUNK, :]
            qn = jnp.concatenate(
                [rope(qn[:, i * LANES:(i + 1) * LANES], cos, sin) for i in range(ATTN_W // LANES)], axis=1)
            kn = rope(kn, cos, sin)
        else:
            k_out_ref[rows(rc), :] = kn
            v_out_ref[rows(rc), :] = vf
        big_scr[rows(rc), 0:ATTN_W] = qn * (HEAD_DIM ** -0.5)
        kall_scr[sq, t0:t0 + ROW_CHUNK, :] = kn
        vall_scr[sq, t0:t0 + ROW_CHUNK, :] = vf
    if latent:
        kall_scr[0, S:lk, :] = ck_ref[0]
        vall_scr[0, S:lk, :] = cvv_ref[0]

    def attend(sq, row0):
        for hd in range(N_HEADS):
            kv = hd // GQ
            qh = big_scr[pl.ds(row0, ROW_CHUNK), hd * HEAD_DIM:(hd + 1) * HEAD_DIM].astype(BF16)
            kh = kall_scr[sq, :, kv * HEAD_DIM:(kv + 1) * HEAD_DIM].astype(BF16)
            vh = vall_scr[sq, :, kv * HEAD_DIM:(kv + 1) * HEAD_DIM].astype(BF16)
            s = lax.dot_general(qh, kh, (((1,), (1,)), ((), ())), preferred_element_type=F32)
            e = jnp.exp(s - jnp.max(s, axis=-1, keepdims=True))
            o = _dot(e.astype(BF16), vh) / jnp.sum(e, axis=-1, keepdims=True)
            big_scr[pl.ds(row0, ROW_CHUNK), ATTN_W + hd * HEAD_DIM:ATTN_W + (hd + 1) * HEAD_DIM] = o

    if latent:
        def attend_chunk(qc, carry):
            attend(0, pl.multiple_of(qc * ROW_CHUNK, ROW_CHUNK))
            return carry
        lax.fori_loop(0, n_rc, attend_chunk, 0)
    else:
        for sq in range(nseq):
            for qc in range(S // ROW_CHUNK):
                attend(sq, sq * S + qc * ROW_CHUNK)

    for rc in range(n_rc):
        hb = h_scr[rows(rc), :]
        ag = _dot(hb, w_in_ref[:, C_AGATE:C_AGATE + ATTN_W])
        t = (big_scr[rows(rc), ATTN_W:2 * ATTN_W] * _silu(ag)).astype(BF16)
        br = _dot(t, w_attn_out_ref[...])
        g = _sigmoid(_dot(hb, w_in_ref[:, C_MERGE + 2 * D_MODEL:C_MERGE + 3 * D_MODEL]))
        mixed = out_ref[rows(rc), :] + g * br
        out_ref[rows(rc), :] = x_ref[rows(rc), :] + gate * _dot(mixed.astype(BF16), w_out_ref[...])


def _const_spec(shape):
    nd = len(shape)
    return pl.BlockSpec(shape, lambda i, _nd=nd: (0,) * _nd, pipeline_mode=pl.Buffered(1))


def _layer_spec(shape, layer):
    nd = len(shape) - 1
    return pl.BlockSpec((None,) + tuple(shape[1:]), lambda i, _nd=nd: (layer,) + (0,) * _nd,
                        pipeline_mode=pl.Buffered(1))


def _layer_call(cfg, layer, x2d, mods, lw, extra):
    R, S, nseq, latent, past = cfg["R"], cfg["S"], cfg["nseq"], cfg["latent"], cfg["past"]
    n_rows = x2d.shape[0]
    n_blocks = n_rows // R
    stacked = [lw["ng"], lw["w_in"], lw["cw"], lw["cvec"], lw["w_conv_out"], lw["bm"], lw["cm"],
               lw["sp_lat" if latent else "sp_ctx"], lw["w_glu"], lw["w_ssm_out"]]
    stacked_tail = [lw["w_attn_out"], lw["w_out"]]
    if latent:
        mods_spec = pl.BlockSpec((None, 1, SUBLANES, D_MODEL), lambda i: (layer, i + 1, 0, 0))
    else:
        mods_spec = pl.BlockSpec((None, 1, SUBLANES, D_MODEL), lambda i: (layer, 0, 0, 0))
    row_mode = dict(pipeline_mode=pl.Buffered(1)) if latent else {}
    in_specs = [pl.BlockSpec((R, D_MODEL), lambda i: (i, 0), **row_mode), mods_spec]
    in_specs += [_layer_spec(w.shape, layer) for w in stacked]
    in_specs += [_const_spec(lw["ones"].shape)]
    in_specs += [_layer_spec(w.shape, layer) for w in stacked_tail]
    args = [x2d, mods] + stacked + [lw["ones"]] + stacked_tail
    out_shape = [jax.ShapeDtypeStruct((n_rows, D_MODEL), F32)]
    out_specs = [pl.BlockSpec((R, D_MODEL), lambda i: (i, 0), **row_mode)]
    if latent:
        cos, sin, ck, cv, h0re, h0im = extra
        in_specs += [_const_spec(cos.shape), _const_spec(sin.shape),
                     pl.BlockSpec((1, None, past, KV_W), lambda i: (i, layer, 0, 0)),
                     pl.BlockSpec((1, None, past, KV_W), lambda i: (i, layer, 0, 0)),
                     pl.BlockSpec((1, None, 2, N_CHUNK, CHUNK_STATES), lambda i: (i, layer, 0, 0, 0)),
                     pl.BlockSpec((1, None, 2, N_CHUNK, CHUNK_STATES), lambda i: (i, layer, 0, 0, 0))]
        args += [cos, sin, ck, cv, h0re, h0im]
    else:
        n_seq_total = n_rows // S
        out_shape += [jax.ShapeDtypeStruct((n_rows, KV_W), F32), jax.ShapeDtypeStruct((n_rows, KV_W), F32),
                      jax.ShapeDtypeStruct((n_seq_total, 2, SSM_G * SSM_P), F32),
                      jax.ShapeDtypeStruct((n_seq_total, 2, SSM_G * SSM_P), F32)]
        out_specs += [pl.BlockSpec((R, KV_W), lambda i: (i, 0)), pl.BlockSpec((R, KV_W), lambda i: (i, 0)),
                      pl.BlockSpec((nseq, 2, SSM_G * SSM_P), lambda i: (i, 0, 0)),
                      pl.BlockSpec((nseq, 2, SSM_G * SSM_P), lambda i: (i, 0, 0))]
    scratch = [
        pltpu.VMEM((R, D_MODEL), BF16),
        pltpu.VMEM((CONV_W // LANES, nseq * (S + 2 * CONV_HALO), LANES), F32),
        pltpu.VMEM((SSM_W // LANES, N_SEG * (cfg["L"] + SEG_PAD), LANES), F32),
        pltpu.VMEM((SSM_W // LANES, R, LANES), F32),
        pltpu.VMEM((SSM_W // LANES, R, LANES), F32),
        pltpu.VMEM((2 if cfg["static_loops"] else 1, R, SCAN_COLS), F32),
        pltpu.VMEM((R, SCAN_COLS), BF16),
        pltpu.VMEM((nseq, S + past, KV_W), F32),
        pltpu.VMEM((nseq, S + past, KV_W), F32),
    ]
    return pl.pallas_call(
        functools.partial(_layer_kernel, cfg),
        grid=(n_blocks,),
        in_specs=in_specs,
        out_specs=out_specs,
        out_shape=out_shape,
        scratch_shapes=scratch,
        compiler_params=pltpu.CompilerParams(
            dimension_semantics=("arbitrary",),
            vmem_limit_bytes=cfg["vmem_bytes"]),
        name="latent_layer" if latent else "context_layer",
    )(*args)


def _ssm_pack(a_re, a_im, log_dt, b_re, b_im, c_re, c_im, seg_len):
    depth = a_re.shape[0]
    lam = lax.complex(a_re, a_im)
    dt = jnp.exp(log_dt)[..., None]
    lam_bar = jnp.exp(lam * dt)
    b_bar = ((lam_bar - 1.0) / lam)[..., None] * lax.complex(b_re, b_im)
    eye = jnp.eye(SCAN_GROUPS, dtype=F32)
    bb = jnp.stack([b_bar.real, b_bar.imag], axis=2)
    bb = bb.reshape(depth, 2, 2, N_CHUNK, SCAN_GROUPS, SSM_P, SSM_GC)
    tb = jnp.einsum("ldacgpk,hg->lchkdagp", bb, eye).reshape(depth, N_CHUNK, SCAN_GROUPS * SSM_GC, SCAN_COLS)
    zb = jnp.zeros_like(tb)
    odd = (jnp.arange(N_CHUNK) % 2 == 1)[None, :, None, None]
    bm = jnp.where(odd, jnp.concatenate([zb, tb], axis=2), jnp.concatenate([tb, zb], axis=2))
    cc = jnp.stack([c_re, -c_im], axis=2)
    cc = cc.reshape(depth, 2, 2, N_CHUNK, SCAN_GROUPS, SSM_GC, SSM_P)
    tc = jnp.einsum("ldacgkp,hg->lcdagphk", cc, eye).reshape(depth, N_CHUNK, SCAN_COLS, SCAN_GROUPS * SSM_GC)
    zc = jnp.zeros_like(tc)
    cm = jnp.where(odd, jnp.concatenate([zc, tc], axis=3), jnp.concatenate([tc, zc], axis=3))

    def chunk_rows(v):
        v = v.reshape(depth, 2, N_CHUNK, CHUNK_STATES).transpose(0, 2, 1, 3)
        return v.reshape(depth, N_CHUNK, 2 * CHUNK_STATES)

    def sp_for(seg):
        pows = [lam_bar] + [jnp.exp(lam * dt * float(seg * d)) for d in (1, 2, 4)]
        rows_ = []
        for p in pows:
            rows_ += [chunk_rows(p.real), chunk_rows(p.imag)]
        return jnp.stack(rows_, axis=2)

    return bm.astype(BF16), cm.astype(BF16), [sp_for(s) for s in seg_len]


def _rope_tables(n_tok):
    rows = n_tok // GRID_W
    row = jnp.repeat(jnp.arange(rows, dtype=F32), GRID_W)
    col = jnp.tile(jnp.arange(GRID_W, dtype=F32), rows)
    inv = ROPE_BASE ** (-jnp.arange(ROPE_PAIRS, dtype=F32) / ROPE_PAIRS)
    ar, ac = row[:, None] * inv[None, :], col[:, None] * inv[None, :]
    ang = jnp.concatenate([ar, ar, ac, ac], axis=-1)
    sign = jnp.tile(jnp.concatenate([-jnp.ones(ROPE_PAIRS, F32), jnp.ones(ROPE_PAIRS, F32)]), 2)
    reps = LANES // HEAD_DIM
    return jnp.tile(jnp.cos(ang), (1, reps)), jnp.tile(jnp.sin(ang) * sign, (1, reps))


def kernel(x_prompt, x_sample, cache_k, cache_v, state_ssm_re, state_ssm_im, c, c_ctx, norm_g, w_ada, b_ada, w_in, conv_dw_w, conv_dw_b, conv_ln_g, conv_ln_b, w_conv_out, ssm_a_re, ssm_a_im, ssm_log_dt, ssm_b_re, ssm_b_im, ssm_c_re, ssm_c_im, ssm_d, w_ssm_glu, w_ssm_out, q_norm_g, k_norm_g, w_attn_out, w_out):
    batch, seq, d_model = x_prompt.shape
    dec_batch, dec_seq, _ = x_sample.shape
    depth = w_in.shape[0]
    past = cache_k.shape[2]
    assert d_model == D_MODEL and w_in.shape[2] == IN_COLS and dec_seq % GRID_W == 0

    ctx_cfg = dict(R=2 * seq, S=seq, nseq=2, latent=False, past=0, static_loops=True,
                   vmem_bytes=52 * 1024 * 1024)
    lat_cfg = dict(R=dec_seq, S=dec_seq, nseq=1, latent=True, past=past, static_loops=False,
                   vmem_bytes=60 * 1024 * 1024)
    for cfg in (ctx_cfg, lat_cfg):
        cfg["L"] = cfg["R"] // N_SEG
        cfg["q"] = cfg["S"] // cfg["L"]
        assert cfg["R"] % ROW_CHUNK == 0 and cfg["S"] % ROW_CHUNK == 0 and cfg["q"] in (1, 2, 4, 8)
        assert cfg["vmem_bytes"] <= VMEM_PHYSICAL_BYTES
        assert ROW_CHUNK % cfg["L"] == 0 and ((cfg["L"] + SEG_PAD) // SUBLANES) % 2 == 1

    cond = jnp.zeros((SUBLANES, d_model), F32).at[0].set(c_ctx).at[1:1 + dec_batch].set(c)
    mods_all = _adaln(cond, w_ada, b_ada)
    cos, sin = _rope_tables(dec_seq)
    head_id = jnp.arange(ATTN_W) // HEAD_DIM
    ones = (head_id[:, None] == head_id[None, :]).astype(BF16)

    bm, cm, (sp_ctx, sp_lat) = _ssm_pack(ssm_a_re, ssm_a_im, ssm_log_dt, ssm_b_re, ssm_b_im, ssm_c_re, ssm_c_im,
                                         (ctx_cfg["L"], lat_cfg["L"]))
    zrow = jnp.zeros((depth, CONV_W), F32)
    cvec = jnp.stack([conv_dw_b, conv_ln_g, conv_ln_b, ssm_d, jnp.tile(q_norm_g, (1, N_HEADS)),
                      jnp.tile(k_norm_g, (1, N_HEADS)), zrow, zrow], axis=1)
    lw = dict(
        ng=norm_g.reshape(depth, 1, d_model), w_in=w_in.astype(BF16),
        cw=jnp.concatenate([conv_dw_w, zrow[:, None, :]], axis=1), cvec=cvec,
        w_conv_out=w_conv_out.astype(BF16), bm=bm, cm=cm, sp_ctx=sp_ctx, sp_lat=sp_lat,
        w_glu=w_ssm_glu.astype(BF16), w_ssm_out=w_ssm_out.astype(BF16), ones=ones,
        w_attn_out=w_attn_out.astype(BF16), w_out=w_out.astype(BF16))
    m3 = mods_all[:, :1 + dec_batch].reshape(depth, 1 + dec_batch, 3, d_model)
    mods = jnp.concatenate(
        [m3, jnp.zeros((depth, 1 + dec_batch, SUBLANES - 3, d_model), F32)], axis=2)
    ck_all = cache_k.reshape(dec_batch, depth, past, KV_W)
    cv_all = cache_v.reshape(dec_batch, depth, past, KV_W)
    h0re = state_ssm_re.reshape(dec_batch, depth, 2, N_CHUNK, CHUNK_STATES)
    h0im = state_ssm_im.reshape(dec_batch, depth, 2, N_CHUNK, CHUNK_STATES)

    yp = x_prompt.reshape(batch * seq, d_model)
    ys = x_sample.reshape(dec_batch * dec_seq, d_model)
    ks_out, vs_out, re_out, im_out = [], [], [], []
    for l in range(depth):
        yp, k_l, v_l, re_l, im_l = _layer_call(ctx_cfg, l, yp, mods, lw, None)
        ks_out.append(k_l.reshape(batch, seq, N_KV_HEADS, HEAD_DIM))
        vs_out.append(v_l.reshape(batch, seq, N_KV_HEADS, HEAD_DIM))
        re_out.append(re_l.reshape(batch, 2, SSM_G, SSM_P))
        im_out.append(im_l.reshape(batch, 2, SSM_G, SSM_P))
        (ys,) = _layer_call(lat_cfg, l, ys, mods, lw, (cos, sin, ck_all, cv_all, h0re, h0im))

    return (yp.reshape(batch, seq, d_model), ys.reshape(dec_batch, dec_seq, d_model),
            jnp.stack(ks_out, axis=1), jnp.stack(vs_out, axis=1),
            jnp.stack(re_out, axis=1), jnp.stack(im_out, axis=1))
```

```python
import functools
import math

import jax
import jax.numpy as jnp
import numpy as np
from jax import lax
from jax.experimental import pallas as pl
from jax.experimental.pallas import tpu as pltpu

F32 = jnp.float32
BF16 = jnp.bfloat16

D_MODEL = 1024
CONV_W = 512
CONV_K = 31
SSM_W = 512
SSM_GC = 16
SSM_G = 32
SSM_P = 64
N_HEADS = 8
N_KV_HEADS = 2
HEAD_DIM = 64
GQ = N_HEADS // N_KV_HEADS
ATTN_W = N_HEADS * HEAD_DIM
KV_W = N_KV_HEADS * HEAD_DIM
ROPE_PAIRS = HEAD_DIM // 4
ROPE_BASE = 10000.0
GRID_W = 64
EPS = 1e-6

C_CONV = 0
C_CGATE = 1024
C_U = 1536
C_UGATE = 2048
C_QKV = 2560
C_AGATE = 3328
C_MERGE = 3840
IN_COLS = 6912

SUBLANES = 8
LANES = 128
N_SEG = SUBLANES
SEG_PAD = 8
SCAN_GROUPS = 4
N_CHUNK = SSM_G // SCAN_GROUPS
CHUNK_STATES = SCAN_GROUPS * SSM_P
SCAN_COLS = 4 * CHUNK_STATES
CONV_TILE = 64
CONV_HALO = 16
ROW_CHUNK = 256
VMEM_PHYSICAL_BYTES = 64 * 1024 * 1024


def _sigmoid(x):
    return jax.nn.sigmoid(x)


def _silu(x):
    return x * jax.nn.sigmoid(x)


def _gelu_tanh(x):
    return 0.5 * x * (1.0 + jnp.tanh(math.sqrt(2.0 / math.pi) * (x + 0.044715 * (x * x * x))))


def _dot(a, b):
    return jnp.dot(a, b, preferred_element_type=F32)


def _aligned(x, m):
    return x if isinstance(x, int) else pl.multiple_of(x, m)


def _for(n, body, carry, *, static, unroll=1):
    if static:
        for i in range(n):
            carry = body(i, carry)
        return carry
    return lax.fori_loop(0, n, body, carry, unroll=unroll)


def _adaln_kernel(cond_ref, w_ref, b_ref, o_ref):
    s = _silu(cond_ref[...])
    o_ref[0] = jnp.dot(s, w_ref[0], preferred_element_type=F32,
                       precision=lax.Precision.HIGHEST) + b_ref[0]


def _adaln(cond, w_ada, b_ada):
    depth, d, n = w_ada.shape
    tn = 512
    return pl.pallas_call(
        _adaln_kernel,
        grid=(depth, n // tn),
        in_specs=[
            pl.BlockSpec((SUBLANES, d), lambda l, j: (0, 0)),
            pl.BlockSpec((1, d, tn), lambda l, j: (l, 0, j)),
            pl.BlockSpec((1, 1, tn), lambda l, j: (l, 0, j)),
        ],
        out_specs=pl.BlockSpec((1, SUBLANES, tn), lambda l, j: (l, 0, j)),
        out_shape=jax.ShapeDtypeStruct((depth, SUBLANES, n), F32),
        name="adaln",
    )(cond, w_ada, b_ada.reshape(depth, 1, n))


def _layer_kernel(cfg, *refs):
    R, S, L, q, nseq, latent, past = (cfg[k] for k in ("R", "S", "L", "q", "nseq", "latent", "past"))
    lk = S + past
    n_rc = R // ROW_CHUNK
    it = iter(refs)
    x_ref, mods_ref, ng_ref, w_in_ref, cw_ref, cvec_ref = (next(it) for _ in range(6))
    w_conv_out_ref, bm_ref, cm_ref, sp_ref, w_glu_ref, w_ssm_out_ref = (next(it) for _ in range(6))
    ones_ref, w_attn_out_ref, w_out_ref = (next(it) for _ in range(3))
    if latent:
        cos_ref, sin_ref, ck_ref, cvv_ref, h0re_ref, h0im_ref = (next(it) for _ in range(6))
    for _ in range(cfg["n_carried"]):
        next(it)
    out_ref = next(it)
    if not latent:
        k_out_ref, v_out_ref, st_re_ref, st_im_ref = (next(it) for _ in range(4))
    h_scr, ypad_scr, u_scr, up_scr, yp_scr, bigs_scr, hs_scr, kall_scr, vall_scr = (next(it) for _ in range(9))
    static_loops = cfg["static_loops"]
    bigs = [bigs_scr.at[i] for i in range(bigs_scr.shape[0])]
    big_scr = bigs[0]
    n_slab = SSM_W // LANES
    seg_pitch = L + SEG_PAD

    ng = ng_ref[...]
    shift = mods_ref[0, 0:1, :]
    scale1 = 1.0 + mods_ref[0, 1:2, :]
    gate = mods_ref[0, 2:3, :]
    pitch = S + 2 * CONV_HALO

    def rows(rc):
        return slice(rc * ROW_CHUNK, (rc + 1) * ROW_CHUNK)

    zero_halo = jnp.zeros((CONV_HALO, LANES), F32)
    for sq in range(nseq):
        for s4 in range(n_slab):
            ypad_scr[s4, sq * pitch:sq * pitch + CONV_HALO, :] = zero_halo
            ypad_scr[s4, sq * pitch + CONV_HALO + S:(sq + 1) * pitch, :] = zero_halo
    for rc in range(n_rc):
        x = x_ref[rows(rc), :]
        ms = jnp.mean(x * x, axis=-1, keepdims=True)
        hn = (x * lax.rsqrt(ms + EPS) * ng) * scale1 + shift
        hb = hn.astype(BF16)
        h_scr[rows(rc), :] = hb
        zc = _dot(hb, w_in_ref[:, C_CONV:C_CONV + 2 * CONV_W])
        y = zc[:, :CONV_W] * _sigmoid(zc[:, CONV_W:])
        r0 = rc * ROW_CHUNK
        sq = r0 // S
        p0 = sq * pitch + CONV_HALO + (r0 - sq * S)
        for s4 in range(n_slab):
            ypad_scr[s4, p0:p0 + ROW_CHUNK, :] = y[:, s4 * LANES:(s4 + 1) * LANES]

    conv_b = cvec_ref[0:1, :]
    ln_g = cvec_ref[1:2, :]
    ln_b = cvec_ref[2:3, :]
    log2_s = int(math.log2(S))

    def conv_tile(t, carry):
        r0 = _aligned(t * CONV_TILE, CONV_TILE)
        sq = (t * CONV_TILE) >> log2_s
        w0 = r0 + sq * (2 * CONV_HALO)
        for s4 in range(n_slab):
            lanes = slice(s4 * LANES, (s4 + 1) * LANES)
            a = jnp.broadcast_to(conv_b[:, lanes], (CONV_TILE, LANES))
            for k in range(CONV_K):
                a = a + cw_ref[k:k + 1, lanes] * ypad_scr[s4, pl.ds(w0 + 1 + k, CONV_TILE, stride=1), :]
            big_scr[pl.ds(r0, CONV_TILE), lanes] = a
        return carry

    _for(R // CONV_TILE, conv_tile, 0, static=static_loops)

    for rc in range(n_rc):
        hb = h_scr[rows(rc), :]
        cg = _dot(hb, w_in_ref[:, C_CGATE:C_CGATE + CONV_W])
        acc = big_scr[rows(rc), 0:CONV_W]
        mu = jnp.mean(acc, axis=-1, keepdims=True)
        xc = acc - mu
        var = jnp.mean(xc * xc, axis=-1, keepdims=True)
        yl = xc * lax.rsqrt(var + EPS) * ln_g + ln_b
        t = (_silu(yl) * _silu(cg)).astype(BF16)
        br = _dot(t, w_conv_out_ref[...])
        g = _sigmoid(_dot(hb, w_in_ref[:, C_MERGE:C_MERGE + D_MODEL]))
        out_ref[rows(rc), :] = g * br

    segs_per_chunk = ROW_CHUNK // L

    def seg_rows(rc, j):
        k = rc * segs_per_chunk + j
        return slice(k * seg_pitch, k * seg_pitch + L)

    for rc in range(n_rc):
        u = _dot(h_scr[rows(rc), :], w_in_ref[:, C_U:C_U + SSM_W])
        for j in range(segs_per_chunk):
            for s4 in range(n_slab):
                u_scr[s4, seg_rows(rc, j), :] = u[j * L:(j + 1) * L, s4 * LANES:(s4 + 1) * LANES]

    def permute_rows(i, carry):
        dst = pl.multiple_of(i * N_SEG, N_SEG)
        for s4 in range(n_slab):
            up_scr[s4, pl.ds(dst, N_SEG), :] = u_scr[s4, pl.ds(i, N_SEG, stride=seg_pitch), :]
        return carry

    lax.fori_loop(0, L, permute_rows, 0)

    tile = (N_SEG, CHUNK_STATES)
    seg_in_seq = lax.broadcasted_iota(jnp.int32, tile, 0) & (q - 1)
    c0, c1, c2, c3, c4 = (i * CHUNK_STATES for i in range(5))

    def cmul(ar, ai, br_, bi_):
        return ar * br_ - ai * bi_, ar * bi_ + ai * br_

    def input_map(c):
        for rc in range(n_rc):
            bigs[c % len(bigs)][rows(rc), :] = _dot(up_scr[c // 2, rows(rc), :].astype(BF16), bm_ref[c])

    lookahead = len(bigs) - 1
    for c in range(lookahead):
        input_map(c)
    for c in range(N_CHUNK):
        slab, half = divmod(c, 2)
        bu_scr = bigs[c % len(bigs)]
        if c + lookahead < N_CHUNK:
            input_map(c + lookahead)

        def sp_row(i, d):
            return jnp.broadcast_to(sp_ref[c, i:i + 1, d * CHUNK_STATES:(d + 1) * CHUNK_STATES], tile)

        lfr, lfi, lbr, lbi = sp_row(0, 0), sp_row(1, 0), sp_row(0, 1), sp_row(1, 1)

        def fwd_step(hr, hi, row):
            row = _aligned(row, N_SEG)
            return (lfr * hr - lfi * hi + bu_scr[pl.ds(row, N_SEG), c0:c1],
                    lfr * hi + lfi * hr + bu_scr[pl.ds(row, N_SEG), c1:c2])

        def bwd_step(hr, hi, row):
            row = _aligned(row, N_SEG)
            return (lbr * hr - lbi * hi + bu_scr[pl.ds(row, N_SEG), c2:c3],
                    lbr * hi + lbi * hr + bu_scr[pl.ds(row, N_SEG), c3:c4])

        def ends_step(i, carry):
            hfr, hfi, hbr, hbi = carry
            hfr, hfi = fwd_step(hfr, hfi, i * N_SEG)
            hbr, hbi = bwd_step(hbr, hbi, (L - 1 - i) * N_SEG)
            return hfr, hfi, hbr, hbi

        z = jnp.zeros(tile, F32)
        efr, efi, ebr, ebi = _for(L, ends_step, (z, z, z, z), static=static_loops, unroll=8)

        if latent:
            h0 = [jnp.broadcast_to(r[0, d, c:c + 1, :], tile) for d in range(2) for r in (h0re_ref, h0im_ref)]
        else:
            h0 = [z, z, z, z]
        hfr = jnp.where(seg_in_seq == 0, h0[0], pltpu.roll(efr, 1, 0))
        hfi = jnp.where(seg_in_seq == 0, h0[1], pltpu.roll(efi, 1, 0))
        hbr = jnp.where(seg_in_seq == q - 1, h0[2], pltpu.roll(ebr, N_SEG - 1, 0))
        hbi = jnp.where(seg_in_seq == q - 1, h0[3], pltpu.roll(ebi, N_SEG - 1, 0))
        for j, dist in enumerate((1, 2, 4)):
            if dist >= q:
                break
            pfr, pfi, pbr, pbi = sp_row(2 + 2 * j, 0), sp_row(3 + 2 * j, 0), sp_row(2 + 2 * j, 1), sp_row(3 + 2 * j, 1)
            ar, ai = cmul(pfr, pfi, pltpu.roll(hfr, dist, 0), pltpu.roll(hfi, dist, 0))
            keep = seg_in_seq >= dist
            hfr, hfi = hfr + jnp.where(keep, ar, 0.0), hfi + jnp.where(keep, ai, 0.0)
            ar, ai = cmul(pbr, pbi, pltpu.roll(hbr, N_SEG - dist, 0), pltpu.roll(hbi, N_SEG - dist, 0))
            keep = seg_in_seq <= q - 1 - dist
            hbr, hbi = hbr + jnp.where(keep, ar, 0.0), hbi + jnp.where(keep, ai, 0.0)

        if not latent:
            plr, pli = sp_row(2, 0), sp_row(3, 0)
            ffr, ffi = cmul(plr, pli, hfr, hfi)
            ffr, ffi = ffr + efr, ffi + efi
            plr, pli = sp_row(2, 1), sp_row(3, 1)
            fbr, fbi = cmul(plr, pli, hbr, hbi)
            fbr, fbi = fbr + ebr, fbi + ebi
            cols = slice(c * CHUNK_STATES, (c + 1) * CHUNK_STATES)
            for sq in range(nseq):
                lf = (sq + 1) * q - 1
                lb = sq * q
                st_re_ref[sq, 0:1, cols] = ffr[lf:lf + 1, :]
                st_im_ref[sq, 0:1, cols] = ffi[lf:lf + 1, :]
                st_re_ref[sq, 1:2, cols] = fbr[lb:lb + 1, :]
                st_im_ref[sq, 1:2, cols] = fbi[lb:lb + 1, :]

        def scan_pair(j, carry):
            hfr, hfi, hbr, hbi = carry
            rf = _aligned(j * (2 * N_SEG), 2 * N_SEG)
            rb = _aligned((L - 2 - 2 * j) * N_SEG, 2 * N_SEG)
            f1r, f1i = fwd_step(hfr, hfi, rf)
            f2r, f2i = fwd_step(f1r, f1i, rf + N_SEG)
            b1r, b1i = bwd_step(hbr, hbi, rb + N_SEG)
            b2r, b2i = bwd_step(b1r, b1i, rb)
            hs_scr[pl.ds(rf, 2 * N_SEG), c0:c1] = jnp.concatenate([f1r, f2r], axis=0).astype(BF16)
            hs_scr[pl.ds(rf, 2 * N_SEG), c1:c2] = jnp.concatenate([f1i, f2i], axis=0).astype(BF16)
            hs_scr[pl.ds(rb, 2 * N_SEG), c2:c3] = jnp.concatenate([b2r, b1r], axis=0).astype(BF16)
            hs_scr[pl.ds(rb, 2 * N_SEG), c3:c4] = jnp.concatenate([b2i, b1i], axis=0).astype(BF16)
            return f2r, f2i, b2r, b2i

        _for(L // 2, scan_pair, (hfr, hfi, hbr, hbi), static=static_loops, unroll=2)

        for rc in range(n_rc):
            yc = _dot(hs_scr[rows(rc), :], cm_ref[c])
            if half == 0:
                yp_scr[slab, rows(rc), :] = yc
            else:
                yp_scr[slab, rows(rc), :] += yc

    def unpermute_rows(j, carry):
        src = j * (N_SEG * SUBLANES)
        for k in range(N_SEG):
            dst = pl.multiple_of(k * L + j * SUBLANES, SUBLANES)
            for s4 in range(n_slab):
                big_scr[pl.ds(dst, SUBLANES), s4 * LANES:(s4 + 1) * LANES] = (
                    yp_scr[s4, pl.ds(src + k, SUBLANES, stride=N_SEG), :])
        return carry

    lax.fori_loop(0, L // SUBLANES, unpermute_rows, 0)

    ssm_d = cvec_ref[3:4, :]
    for rc in range(n_rc):
        hb = h_scr[rows(rc), :]
        u = jnp.concatenate(
            [jnp.concatenate([u_scr[s4, seg_rows(rc, j), :] for j in range(segs_per_chunk)], axis=0)
             for s4 in range(n_slab)], axis=1)
        yb = big_scr[rows(rc), 0:SSM_W] + ssm_d * u
        z2 = _dot(_gelu_tanh(yb).astype(BF16), w_glu_ref[...])
        yb = z2[:, :SSM_W] * _sigmoid(z2[:, SSM_W:])
        sg = _dot(hb, w_in_ref[:, C_UGATE:C_UGATE + SSM_W])
        t = (yb * _silu(sg)).astype(BF16)
        br = _dot(t, w_ssm_out_ref[...])
        g = _sigmoid(_dot(hb, w_in_ref[:, C_MERGE + D_MODEL:C_MERGE + 2 * D_MODEL]))
        out_ref[rows(rc), :] += g * br

    q_g = cvec_ref[4:5, :]
    k_g = cvec_ref[5:6, 0:KV_W]
    lane = lax.broadcasted_iota(jnp.int32, (ROW_CHUNK, LANES), 1)
    first_half = (lane & (2 * ROPE_PAIRS - 1)) < ROPE_PAIRS

    def rope(xs, cos, sin):
        swapped = jnp.where(first_half, pltpu.roll(xs, LANES - ROPE_PAIRS, 1), pltpu.roll(xs, ROPE_PAIRS, 1))
        return xs * cos + swapped * sin

    def head_ms(x2, n):
        hi = x2.astype(BF16)
        lo = (x2 - hi.astype(F32)).astype(BF16)
        ones = ones_ref[0:n, 0:n]
        return (_dot(hi, ones) + _dot(lo, ones)) * (1.0 / HEAD_DIM)

    for rc in range(n_rc):
        zq = _dot(h_scr[rows(rc), :], w_in_ref[:, C_QKV:C_QKV + ATTN_W + 2 * KV_W])
        qf, kf, vf = zq[:, :ATTN_W], zq[:, ATTN_W:ATTN_W + KV_W], zq[:, ATTN_W + KV_W:]
        qn = qf * lax.rsqrt(head_ms(qf * qf, ATTN_W) + EPS) * q_g
        kn = kf * lax.rsqrt(head_ms(kf * kf, KV_W) + EPS) * k_g
        r0 = rc * ROW_CHUNK
        sq = r0 // S
        t0 = r0 - sq * S
        if latent:
            cos = cos_ref[t0:t0 + ROW_CHUNK, :]
            sin = sin_ref[t0:t0 + ROW_CHUNK, :]
            qn = jnp.concatenate(
                [rope(qn[:, i * LANES:(i + 1) * LANES], cos, sin) for i in range(ATTN_W // LANES)], axis=1)
            kn = rope(kn, cos, sin)
        else:
            k_out_ref[sq, t0:t0 + ROW_CHUNK, :] = kn
            v_out_ref[sq, t0:t0 + ROW_CHUNK, :] = vf
        big_scr[rows(rc), 0:ATTN_W] = qn * (HEAD_DIM ** -0.5)
        kall_scr[sq, t0:t0 + ROW_CHUNK, :] = kn
        vall_scr[sq, t0:t0 + ROW_CHUNK, :] = vf
    if latent:
        kall_scr[0, S:lk, :] = ck_ref[0]
        vall_scr[0, S:lk, :] = cvv_ref[0]

    def attend(sq, row0):
        for hd in range(N_HEADS):
            kv = hd // GQ
            qh = big_scr[pl.ds(row0, ROW_CHUNK), hd * HEAD_DIM:(hd + 1) * HEAD_DIM].astype(BF16)
            kh = kall_scr[sq, :, kv * HEAD_DIM:(kv + 1) * HEAD_DIM].astype(BF16)
            vh = vall_scr[sq, :, kv * HEAD_DIM:(kv + 1) * HEAD_DIM].astype(BF16)
            s = lax.dot_general(qh, kh, (((1,), (1,)), ((), ())), preferred_element_type=F32)
            e = jnp.exp(s - jnp.max(s, axis=-1, keepdims=True))
            o = _dot(e.astype(BF16), vh) / jnp.sum(e, axis=-1, keepdims=True)
            big_scr[pl.ds(row0, ROW_CHUNK), ATTN_W + hd * HEAD_DIM:ATTN_W + (hd + 1) * HEAD_DIM] = o

    if latent:
        def attend_chunk(qc, carry):
            attend(0, pl.multiple_of(qc * ROW_CHUNK, ROW_CHUNK))
            return carry
        lax.fori_loop(0, n_rc, attend_chunk, 0)
    else:
        for sq in range(nseq):
            for qc in range(S // ROW_CHUNK):
                attend(sq, sq * S + qc * ROW_CHUNK)

    for rc in range(n_rc):
        hb = h_scr[rows(rc), :]
        ag = _dot(hb, w_in_ref[:, C_AGATE:C_AGATE + ATTN_W])
        t = (big_scr[rows(rc), ATTN_W:2 * ATTN_W] * _silu(ag)).astype(BF16)
        br = _dot(t, w_attn_out_ref[...])
        g = _sigmoid(_dot(hb, w_in_ref[:, C_MERGE + 2 * D_MODEL:C_MERGE + 3 * D_MODEL]))
        mixed = out_ref[rows(rc), :] + g * br
        out_ref[rows(rc), :] = x_ref[rows(rc), :] + gate * _dot(mixed.astype(BF16), w_out_ref[...])


def _const_spec(shape):
    nd = len(shape)
    return pl.BlockSpec(shape, lambda i, _nd=nd: (0,) * _nd, pipeline_mode=pl.Buffered(1))


def _layer_spec(shape, layer):
    nd = len(shape) - 1
    return pl.BlockSpec((None,) + tuple(shape[1:]), lambda i, _nd=nd: (layer,) + (0,) * _nd,
                        pipeline_mode=pl.Buffered(1))


def _layer_call(cfg, layer, x2d, mods, lw, extra):
    R, S, nseq, latent, past = cfg["R"], cfg["S"], cfg["nseq"], cfg["latent"], cfg["past"]
    n_rows = x2d.shape[0]
    n_blocks = n_rows // R
    stacked = [lw["ng"], lw["w_in"], lw["cw"], lw["cvec"], lw["w_conv_out"], lw["bm"], lw["cm"],
               lw["sp_lat" if latent else "sp_ctx"], lw["w_glu"], lw["w_ssm_out"]]
    stacked_tail = [lw["w_attn_out"], lw["w_out"]]
    if latent:
        mods_spec = pl.BlockSpec((None, 1, SUBLANES, D_MODEL), lambda i: (layer, i + 1, 0, 0))
    else:
        mods_spec = pl.BlockSpec((None, 1, SUBLANES, D_MODEL), lambda i: (layer, 0, 0, 0))
    row_mode = dict(pipeline_mode=pl.Buffered(1)) if latent else {}
    in_specs = [pl.BlockSpec((R, D_MODEL), lambda i: (i, 0), **row_mode), mods_spec]
    in_specs += [_layer_spec(w.shape, layer) for w in stacked]
    in_specs += [_const_spec(lw["ones"].shape)]
    in_specs += [_layer_spec(w.shape, layer) for w in stacked_tail]
    args = [x2d, mods] + stacked + [lw["ones"]] + stacked_tail
    out_shape = [jax.ShapeDtypeStruct((n_rows, D_MODEL), F32)]
    out_specs = [pl.BlockSpec((R, D_MODEL), lambda i: (i, 0), **row_mode)]
    if latent:
        cos, sin, ck, cv, h0re, h0im = extra
        in_specs += [_const_spec(cos.shape), _const_spec(sin.shape),
                     pl.BlockSpec((1, None, past, KV_W), lambda i: (i, layer, 0, 0)),
                     pl.BlockSpec((1, None, past, KV_W), lambda i: (i, layer, 0, 0)),
                     pl.BlockSpec((1, None, 2, N_CHUNK, CHUNK_STATES), lambda i: (i, layer, 0, 0, 0)),
                     pl.BlockSpec((1, None, 2, N_CHUNK, CHUNK_STATES), lambda i: (i, layer, 0, 0, 0))]
        args += [cos, sin, ck, cv, h0re, h0im]
    aliases = {}
    carried = ()
    if not latent:
        n_seq_total = n_rows // S
        depth = lw["w_in"].shape[0]
        side_shapes = [(n_seq_total, depth, S, KV_W)] * 2 + [(n_seq_total, depth, 2, SSM_G * SSM_P)] * 2
        out_shape += [jax.ShapeDtypeStruct(s, F32) for s in side_shapes]
        out_specs += [pl.BlockSpec((nseq, None) + s[2:], lambda i: (i, layer, 0, 0)) for s in side_shapes]
        carried = tuple(extra) if extra is not None else ()
        aliases = {len(args) + j: 1 + j for j in range(len(carried))}
        in_specs += [pl.BlockSpec(memory_space=pl.ANY) for _ in carried]
        args += list(carried)
    cfg = dict(cfg, n_carried=len(carried))
    scratch = [
        pltpu.VMEM((R, D_MODEL), BF16),
        pltpu.VMEM((CONV_W // LANES, nseq * (S + 2 * CONV_HALO), LANES), F32),
        pltpu.VMEM((SSM_W // LANES, N_SEG * (cfg["L"] + SEG_PAD), LANES), F32),
        pltpu.VMEM((SSM_W // LANES, R, LANES), F32),
        pltpu.VMEM((SSM_W // LANES, R, LANES), F32),
        pltpu.VMEM((2 if cfg["static_loops"] else 1, R, SCAN_COLS), F32),
        pltpu.VMEM((R, SCAN_COLS), BF16),
        pltpu.VMEM((nseq, S + past, KV_W), F32),
        pltpu.VMEM((nseq, S + past, KV_W), F32),
    ]
    return pl.pallas_call(
        functools.partial(_layer_kernel, cfg),
        grid=(n_blocks,),
        in_specs=in_specs,
        out_specs=out_specs,
        out_shape=out_shape,
        scratch_shapes=scratch,
        input_output_aliases=aliases,
        compiler_params=pltpu.CompilerParams(
            dimension_semantics=("arbitrary",),
            vmem_limit_bytes=cfg["vmem_bytes"]),
        name="latent_layer" if latent else "context_layer",
    )(*args)


def _ssm_pack(a_re, a_im, log_dt, b_re, b_im, c_re, c_im, seg_len):
    depth = a_re.shape[0]
    lam = lax.complex(a_re, a_im)
    dt = jnp.exp(log_dt)[..., None]
    lam_bar = jnp.exp(lam * dt)
    b_bar = ((lam_bar - 1.0) / lam)[..., None] * lax.complex(b_re, b_im)
    eye = jnp.eye(SCAN_GROUPS, dtype=F32)
    bb = jnp.stack([b_bar.real, b_bar.imag], axis=2)
    bb = bb.reshape(depth, 2, 2, N_CHUNK, SCAN_GROUPS, SSM_P, SSM_GC)
    tb = jnp.einsum("ldacgpk,hg->lchkdagp", bb, eye).reshape(depth, N_CHUNK, SCAN_GROUPS * SSM_GC, SCAN_COLS)
    zb = jnp.zeros_like(tb)
    odd = (jnp.arange(N_CHUNK) % 2 == 1)[None, :, None, None]
    bm = jnp.where(odd, jnp.concatenate([zb, tb], axis=2), jnp.concatenate([tb, zb], axis=2))
    cc = jnp.stack([c_re, -c_im], axis=2)
    cc = cc.reshape(depth, 2, 2, N_CHUNK, SCAN_GROUPS, SSM_GC, SSM_P)
    tc = jnp.einsum("ldacgkp,hg->lcdagphk", cc, eye).reshape(depth, N_CHUNK, SCAN_COLS, SCAN_GROUPS * SSM_GC)
    zc = jnp.zeros_like(tc)
    cm = jnp.where(odd, jnp.concatenate([zc, tc], axis=3), jnp.concatenate([tc, zc], axis=3))

    def chunk_rows(v):
        v = v.reshape(depth, 2, N_CHUNK, CHUNK_STATES).transpose(0, 2, 1, 3)
        return v.reshape(depth, N_CHUNK, 2 * CHUNK_STATES)

    def sp_for(seg):
        pows = [lam_bar] + [jnp.exp(lam * dt * float(seg * d)) for d in (1, 2, 4)]
        rows_ = []
        for p in pows:
            rows_ += [chunk_rows(p.real), chunk_rows(p.imag)]
        return jnp.stack(rows_, axis=2)

    return bm.astype(BF16), cm.astype(BF16), [sp_for(s) for s in seg_len]


def _rope_tables(n_tok):
    rows = n_tok // GRID_W
    row = jnp.repeat(jnp.arange(rows, dtype=F32), GRID_W)
    col = jnp.tile(jnp.arange(GRID_W, dtype=F32), rows)
    inv = ROPE_BASE ** (-jnp.arange(ROPE_PAIRS, dtype=F32) / ROPE_PAIRS)
    ar, ac = row[:, None] * inv[None, :], col[:, None] * inv[None, :]
    ang = jnp.concatenate([ar, ar, ac, ac], axis=-1)
    sign = jnp.tile(jnp.concatenate([-jnp.ones(ROPE_PAIRS, F32), jnp.ones(ROPE_PAIRS, F32)]), 2)
    reps = LANES // HEAD_DIM
    return jnp.tile(jnp.cos(ang), (1, reps)), jnp.tile(jnp.sin(ang) * sign, (1, reps))


def kernel(x_prompt, x_sample, cache_k, cache_v, state_ssm_re, state_ssm_im, c, c_ctx, norm_g, w_ada, b_ada, w_in, conv_dw_w, conv_dw_b, conv_ln_g, conv_ln_b, w_conv_out, ssm_a_re, ssm_a_im, ssm_log_dt, ssm_b_re, ssm_b_im, ssm_c_re, ssm_c_im, ssm_d, w_ssm_glu, w_ssm_out, q_norm_g, k_norm_g, w_attn_out, w_out):
    batch, seq, d_model = x_prompt.shape
    dec_batch, dec_seq, _ = x_sample.shape
    depth = w_in.shape[0]
    past = cache_k.shape[2]
    assert d_model == D_MODEL and w_in.shape[2] == IN_COLS and dec_seq % GRID_W == 0

    ctx_cfg = dict(R=2 * seq, S=seq, nseq=2, latent=False, past=0, static_loops=True,
                   vmem_bytes=52 * 1024 * 1024)
    lat_cfg = dict(R=dec_seq, S=dec_seq, nseq=1, latent=True, past=past, static_loops=False,
                   vmem_bytes=60 * 1024 * 1024)
    for cfg in (ctx_cfg, lat_cfg):
        cfg["L"] = cfg["R"] // N_SEG
        cfg["q"] = cfg["S"] // cfg["L"]
        assert cfg["R"] % ROW_CHUNK == 0 and cfg["S"] % ROW_CHUNK == 0 and cfg["q"] in (1, 2, 4, 8)
        assert cfg["vmem_bytes"] <= VMEM_PHYSICAL_BYTES
        assert ROW_CHUNK % cfg["L"] == 0 and ((cfg["L"] + SEG_PAD) // SUBLANES) % 2 == 1

    cond = jnp.zeros((SUBLANES, d_model), F32).at[0].set(c_ctx).at[1:1 + dec_batch].set(c)
    mods_all = _adaln(cond, w_ada, b_ada)
    cos, sin = _rope_tables(dec_seq)
    head_id = jnp.arange(ATTN_W) // HEAD_DIM
    ones = (head_id[:, None] == head_id[None, :]).astype(BF16)

    bm, cm, (sp_ctx, sp_lat) = _ssm_pack(ssm_a_re, ssm_a_im, ssm_log_dt, ssm_b_re, ssm_b_im, ssm_c_re, ssm_c_im,
                                         (ctx_cfg["L"], lat_cfg["L"]))
    zrow = jnp.zeros((depth, CONV_W), F32)
    cvec = jnp.stack([conv_dw_b, conv_ln_g, conv_ln_b, ssm_d, jnp.tile(q_norm_g, (1, N_HEADS)),
                      jnp.tile(k_norm_g, (1, N_HEADS)), zrow, zrow], axis=1)
    lw = dict(
        ng=norm_g.reshape(depth, 1, d_model), w_in=w_in.astype(BF16),
        cw=jnp.concatenate([conv_dw_w, zrow[:, None, :]], axis=1), cvec=cvec,
        w_conv_out=w_conv_out.astype(BF16), bm=bm, cm=cm, sp_ctx=sp_ctx, sp_lat=sp_lat,
        w_glu=w_ssm_glu.astype(BF16), w_ssm_out=w_ssm_out.astype(BF16), ones=ones,
        w_attn_out=w_attn_out.astype(BF16), w_out=w_out.astype(BF16))
    m3 = mods_all[:, :1 + dec_batch].reshape(depth, 1 + dec_batch, 3, d_model)
    mods = jnp.concatenate(
        [m3, jnp.zeros((depth, 1 + dec_batch, SUBLANES - 3, d_model), F32)], axis=2)
    ck_all = cache_k.reshape(dec_batch, depth, past, KV_W)
    cv_all = cache_v.reshape(dec_batch, depth, past, KV_W)
    h0re = state_ssm_re.reshape(dec_batch, depth, 2, N_CHUNK, CHUNK_STATES)
    h0im = state_ssm_im.reshape(dec_batch, depth, 2, N_CHUNK, CHUNK_STATES)

    yp = x_prompt.reshape(batch * seq, d_model)
    ys = x_sample.reshape(dec_batch * dec_seq, d_model)
    side = None
    for l in range(depth):
        yp, *side = _layer_call(ctx_cfg, l, yp, mods, lw, side)
        (ys,) = _layer_call(lat_cfg, l, ys, mods, lw, (cos, sin, ck_all, cv_all, h0re, h0im))
    k_new, v_new, st_re, st_im = side

    return (yp.reshape(batch, seq, d_model), ys.reshape(dec_batch, dec_seq, d_model),
            k_new.reshape(batch, depth, seq, N_KV_HEADS, HEAD_DIM),
            v_new.reshape(batch, depth, seq, N_KV_HEADS, HEAD_DIM),
            st_re.reshape(batch, depth, 2, SSM_G, SSM_P), st_im.reshape(batch, depth, 2, SSM_G, SSM_P))
```

```python
import functools
import math

import jax
import jax.numpy as jnp
import numpy as np
from jax import lax
from jax.experimental import pallas as pl
from jax.experimental.pallas import tpu as pltpu

F32 = jnp.float32
BF16 = jnp.bfloat16

D_MODEL = 1024
CONV_W = 512
CONV_K = 31
SSM_W = 512
SSM_GC = 16
SSM_G = 32
SSM_P = 64
N_HEADS = 8
N_KV_HEADS = 2
HEAD_DIM = 64
GQ = N_HEADS // N_KV_HEADS
ATTN_W = N_HEADS * HEAD_DIM
KV_W = N_KV_HEADS * HEAD_DIM
ROPE_PAIRS = HEAD_DIM // 4
ROPE_BASE = 10000.0
GRID_W = 64
EPS = 1e-6

C_CONV = 0
C_CGATE = 1024
C_U = 1536
C_UGATE = 2048
C_QKV = 2560
C_AGATE = 3328
C_MERGE = 3840
IN_COLS = 6912

SUBLANES = 8
LANES = 128
N_SEG = SUBLANES
SEG_PAD = 8
SCAN_GROUPS = 4
N_CHUNK = SSM_G // SCAN_GROUPS
CHUNK_STATES = SCAN_GROUPS * SSM_P
SCAN_COLS = 4 * CHUNK_STATES
CONV_TILE = 64
CONV_HALO = 16
ROW_CHUNK = 256
VMEM_PHYSICAL_BYTES = 64 * 1024 * 1024


def _sigmoid(x):
    return jax.nn.sigmoid(x)


def _silu(x):
    return x * jax.nn.sigmoid(x)


def _gelu_tanh(x):
    return 0.5 * x * (1.0 + jnp.tanh(math.sqrt(2.0 / math.pi) * (x + 0.044715 * (x * x * x))))


def _dot(a, b):
    return jnp.dot(a, b, preferred_element_type=F32)


def _aligned(x, m):
    return x if isinstance(x, int) else pl.multiple_of(x, m)


def _for(n, body, carry, *, static, unroll=1):
    if static:
        for i in range(n):
            carry = body(i, carry)
        return carry
    return lax.fori_loop(0, n, body, carry, unroll=unroll)


def _adaln_kernel(cond_ref, w_ref, b_ref, o_ref):
    s = _silu(cond_ref[...])
    o_ref[0] = jnp.dot(s, w_ref[0], preferred_element_type=F32,
                       precision=lax.Precision.HIGHEST) + b_ref[0]


def _adaln(cond, w_ada, b_ada):
    depth, d, n = w_ada.shape
    tn = 512
    return pl.pallas_call(
        _adaln_kernel,
        grid=(depth, n // tn),
        in_specs=[
            pl.BlockSpec((SUBLANES, d), lambda l, j: (0, 0)),
            pl.BlockSpec((1, d, tn), lambda l, j: (l, 0, j)),
            pl.BlockSpec((1, 1, tn), lambda l, j: (l, 0, j)),
        ],
        out_specs=pl.BlockSpec((1, SUBLANES, tn), lambda l, j: (l, 0, j)),
        out_shape=jax.ShapeDtypeStruct((depth, SUBLANES, n), F32),
        name="adaln",
    )(cond, w_ada, b_ada.reshape(depth, 1, n))


def _layer_kernel(cfg, *refs):
    R, S, L, q, nseq, latent, past = (cfg[k] for k in ("R", "S", "L", "q", "nseq", "latent", "past"))
    lk = S + past
    n_rc = R // ROW_CHUNK
    it = iter(refs)
    x_ref, mods_ref, ng_ref, w_in_ref, cw_ref, cvec_ref = (next(it) for _ in range(6))
    w_conv_out_ref, bm_ref, cm_ref, sp_ref, w_glu_ref, w_ssm_out_ref = (next(it) for _ in range(6))
    ones_ref, w_attn_out_ref, w_out_ref = (next(it) for _ in range(3))
    if latent:
        cos_ref, sin_ref, ck_ref, cvv_ref, h0re_ref, h0im_ref = (next(it) for _ in range(6))
    for _ in range(cfg["n_carried"]):
        next(it)
    out_ref = next(it)
    if not latent:
        k_out_ref, v_out_ref, st_re_ref, st_im_ref = (next(it) for _ in range(4))
    h_scr, ypad_scr, u_scr, up_scr, yp_scr, bigs_scr, hs_scr, kall_scr, vall_scr = (next(it) for _ in range(9))
    static_loops = cfg["static_loops"]
    bigs = [bigs_scr.at[i] for i in range(bigs_scr.shape[0])]
    big_scr = bigs[0]
    n_slab = SSM_W // LANES
    seg_pitch = L + SEG_PAD

    ng = ng_ref[...]
    shift = mods_ref[0, 0:1, :]
    scale1 = 1.0 + mods_ref[0, 1:2, :]
    gate = mods_ref[0, 2:3, :]
    pitch = S + 2 * CONV_HALO

    def rows(rc):
        return slice(rc * ROW_CHUNK, (rc + 1) * ROW_CHUNK)

    zero_halo = jnp.zeros((CONV_HALO, LANES), F32)
    for sq in range(nseq):
        for s4 in range(n_slab):
            ypad_scr[s4, sq * pitch:sq * pitch + CONV_HALO, :] = zero_halo
            ypad_scr[s4, sq * pitch + CONV_HALO + S:(sq + 1) * pitch, :] = zero_halo
    for rc in range(n_rc):
        x = x_ref[rows(rc), :]
        ms = jnp.mean(x * x, axis=-1, keepdims=True)
        hn = (x * lax.rsqrt(ms + EPS) * ng) * scale1 + shift
        hb = hn.astype(BF16)
        h_scr[rows(rc), :] = hb
        zc = _dot(hb, w_in_ref[:, C_CONV:C_CONV + 2 * CONV_W])
        y = zc[:, :CONV_W] * _sigmoid(zc[:, CONV_W:])
        r0 = rc * ROW_CHUNK
        sq = r0 // S
        p0 = sq * pitch + CONV_HALO + (r0 - sq * S)
        for s4 in range(n_slab):
            ypad_scr[s4, p0:p0 + ROW_CHUNK, :] = y[:, s4 * LANES:(s4 + 1) * LANES]

    conv_b = cvec_ref[0:1, :]
    ln_g = cvec_ref[1:2, :]
    ln_b = cvec_ref[2:3, :]
    log2_s = int(math.log2(S))

    def conv_tile(t, carry):
        r0 = _aligned(t * CONV_TILE, CONV_TILE)
        sq = (t * CONV_TILE) >> log2_s
        w0 = r0 + sq * (2 * CONV_HALO)
        for s4 in range(n_slab):
            lanes = slice(s4 * LANES, (s4 + 1) * LANES)
            a = jnp.broadcast_to(conv_b[:, lanes], (CONV_TILE, LANES))
            for k in range(CONV_K):
                a = a + cw_ref[k:k + 1, lanes] * ypad_scr[s4, pl.ds(w0 + 1 + k, CONV_TILE, stride=1), :]
            big_scr[pl.ds(r0, CONV_TILE), lanes] = a
        return carry

    _for(R // CONV_TILE, conv_tile, 0, static=static_loops)

    for rc in range(n_rc):
        hb = h_scr[rows(rc), :]
        cg = _dot(hb, w_in_ref[:, C_CGATE:C_CGATE + CONV_W])
        acc = big_scr[rows(rc), 0:CONV_W]
        mu = jnp.mean(acc, axis=-1, keepdims=True)
        xc = acc - mu
        var = jnp.mean(xc * xc, axis=-1, keepdims=True)
        yl = xc * lax.rsqrt(var + EPS) * ln_g + ln_b
        t = (_silu(yl) * _silu(cg)).astype(BF16)
        br = _dot(t, w_conv_out_ref[...])
        g = _sigmoid(_dot(hb, w_in_ref[:, C_MERGE:C_MERGE + D_MODEL]))
        out_ref[rows(rc), :] = g * br

    segs_per_chunk = ROW_CHUNK // L

    def seg_rows(rc, j):
        k = rc * segs_per_chunk + j
        return slice(k * seg_pitch, k * seg_pitch + L)

    for rc in range(n_rc):
        u = _dot(h_scr[rows(rc), :], w_in_ref[:, C_U:C_U + SSM_W])
        for j in range(segs_per_chunk):
            for s4 in range(n_slab):
                u_scr[s4, seg_rows(rc, j), :] = u[j * L:(j + 1) * L, s4 * LANES:(s4 + 1) * LANES]

    def permute_rows(i, carry):
        dst = _aligned(i * N_SEG, N_SEG)
        for s4 in range(n_slab):
            up_scr[s4, pl.ds(dst, N_SEG), :] = u_scr[s4, pl.ds(i, N_SEG, stride=seg_pitch), :]
        return carry

    _for(L, permute_rows, 0, static=static_loops)

    tile = (N_SEG, CHUNK_STATES)
    seg_in_seq = lax.broadcasted_iota(jnp.int32, tile, 0) & (q - 1)
    c0, c1, c2, c3, c4 = (i * CHUNK_STATES for i in range(5))

    def cmul(ar, ai, br_, bi_):
        return ar * br_ - ai * bi_, ar * bi_ + ai * br_

    def input_map(c):
        for rc in range(n_rc):
            bigs[c % len(bigs)][rows(rc), :] = _dot(up_scr[c // 2, rows(rc), :].astype(BF16), bm_ref[c])

    lookahead = len(bigs) - 1
    for c in range(lookahead):
        input_map(c)
    for c in range(N_CHUNK):
        slab, half = divmod(c, 2)
        bu_scr = bigs[c % len(bigs)]
        if c + lookahead < N_CHUNK:
            input_map(c + lookahead)

        def sp_row(i, d):
            return jnp.broadcast_to(sp_ref[c, i:i + 1, d * CHUNK_STATES:(d + 1) * CHUNK_STATES], tile)

        lfr, lfi, lbr, lbi = sp_row(0, 0), sp_row(1, 0), sp_row(0, 1), sp_row(1, 1)

        def fwd_step(hr, hi, row):
            row = _aligned(row, N_SEG)
            return (lfr * hr - lfi * hi + bu_scr[pl.ds(row, N_SEG), c0:c1],
                    lfr * hi + lfi * hr + bu_scr[pl.ds(row, N_SEG), c1:c2])

        def bwd_step(hr, hi, row):
            row = _aligned(row, N_SEG)
            return (lbr * hr - lbi * hi + bu_scr[pl.ds(row, N_SEG), c2:c3],
                    lbr * hi + lbi * hr + bu_scr[pl.ds(row, N_SEG), c3:c4])

        def ends_step(i, carry):
            hfr, hfi, hbr, hbi = carry
            hfr, hfi = fwd_step(hfr, hfi, i * N_SEG)
            hbr, hbi = bwd_step(hbr, hbi, (L - 1 - i) * N_SEG)
            return hfr, hfi, hbr, hbi

        z = jnp.zeros(tile, F32)
        efr, efi, ebr, ebi = _for(L, ends_step, (z, z, z, z), static=static_loops, unroll=8)

        if latent:
            h0 = [jnp.broadcast_to(r[0, d, c:c + 1, :], tile) for d in range(2) for r in (h0re_ref, h0im_ref)]
        else:
            h0 = [z, z, z, z]
        hfr = jnp.where(seg_in_seq == 0, h0[0], pltpu.roll(efr, 1, 0))
        hfi = jnp.where(seg_in_seq == 0, h0[1], pltpu.roll(efi, 1, 0))
        hbr = jnp.where(seg_in_seq == q - 1, h0[2], pltpu.roll(ebr, N_SEG - 1, 0))
        hbi = jnp.where(seg_in_seq == q - 1, h0[3], pltpu.roll(ebi, N_SEG - 1, 0))
        for j, dist in enumerate((1, 2, 4)):
            if dist >= q:
                break
            pfr, pfi, pbr, pbi = sp_row(2 + 2 * j, 0), sp_row(3 + 2 * j, 0), sp_row(2 + 2 * j, 1), sp_row(3 + 2 * j, 1)
            ar, ai = cmul(pfr, pfi, pltpu.roll(hfr, dist, 0), pltpu.roll(hfi, dist, 0))
            keep = seg_in_seq >= dist
            hfr, hfi = hfr + jnp.where(keep, ar, 0.0), hfi + jnp.where(keep, ai, 0.0)
            ar, ai = cmul(pbr, pbi, pltpu.roll(hbr, N_SEG - dist, 0), pltpu.roll(hbi, N_SEG - dist, 0))
            keep = seg_in_seq <= q - 1 - dist
            hbr, hbi = hbr + jnp.where(keep, ar, 0.0), hbi + jnp.where(keep, ai, 0.0)

        if not latent:
            plr, pli = sp_row(2, 0), sp_row(3, 0)
            ffr, ffi = cmul(plr, pli, hfr, hfi)
            ffr, ffi = ffr + efr, ffi + efi
            plr, pli = sp_row(2, 1), sp_row(3, 1)
            fbr, fbi = cmul(plr, pli, hbr, hbi)
            fbr, fbi = fbr + ebr, fbi + ebi
            cols = slice(c * CHUNK_STATES, (c + 1) * CHUNK_STATES)
            for sq in range(nseq):
                lf = (sq + 1) * q - 1
                lb = sq * q
                st_re_ref[sq, 0:1, cols] = ffr[lf:lf + 1, :]
                st_im_ref[sq, 0:1, cols] = ffi[lf:lf + 1, :]
                st_re_ref[sq, 1:2, cols] = fbr[lb:lb + 1, :]
                st_im_ref[sq, 1:2, cols] = fbi[lb:lb + 1, :]

        def scan_pair(j, carry):
            hfr, hfi, hbr, hbi = carry
            rf = _aligned(j * (2 * N_SEG), 2 * N_SEG)
            rb = _aligned((L - 2 - 2 * j) * N_SEG, 2 * N_SEG)
            f1r, f1i = fwd_step(hfr, hfi, rf)
            f2r, f2i = fwd_step(f1r, f1i, rf + N_SEG)
            b1r, b1i = bwd_step(hbr, hbi, rb + N_SEG)
            b2r, b2i = bwd_step(b1r, b1i, rb)
            hs_scr[pl.ds(rf, 2 * N_SEG), c0:c1] = jnp.concatenate([f1r, f2r], axis=0).astype(BF16)
            hs_scr[pl.ds(rf, 2 * N_SEG), c1:c2] = jnp.concatenate([f1i, f2i], axis=0).astype(BF16)
            hs_scr[pl.ds(rb, 2 * N_SEG), c2:c3] = jnp.concatenate([b2r, b1r], axis=0).astype(BF16)
            hs_scr[pl.ds(rb, 2 * N_SEG), c3:c4] = jnp.concatenate([b2i, b1i], axis=0).astype(BF16)
            return f2r, f2i, b2r, b2i

        _for(L // 2, scan_pair, (hfr, hfi, hbr, hbi), static=static_loops, unroll=2)

        for rc in range(n_rc):
            yc = _dot(hs_scr[rows(rc), :], cm_ref[c])
            if half == 0:
                yp_scr[slab, rows(rc), :] = yc
            else:
                yp_scr[slab, rows(rc), :] += yc

    def unpermute_rows(j, carry):
        src = j * (N_SEG * SUBLANES)
        for k in range(N_SEG):
            dst = _aligned(k * L + j * SUBLANES, SUBLANES)
            for s4 in range(n_slab):
                big_scr[pl.ds(dst, SUBLANES), s4 * LANES:(s4 + 1) * LANES] = (
                    yp_scr[s4, pl.ds(src + k, SUBLANES, stride=N_SEG), :])
        return carry

    _for(L // SUBLANES, unpermute_rows, 0, static=static_loops)

    ssm_d = cvec_ref[3:4, :]
    for rc in range(n_rc):
        hb = h_scr[rows(rc), :]
        u = jnp.concatenate(
            [jnp.concatenate([u_scr[s4, seg_rows(rc, j), :] for j in range(segs_per_chunk)], axis=0)
             for s4 in range(n_slab)], axis=1)
        yb = big_scr[rows(rc), 0:SSM_W] + ssm_d * u
        z2 = _dot(_gelu_tanh(yb).astype(BF16), w_glu_ref[...])
        yb = z2[:, :SSM_W] * _sigmoid(z2[:, SSM_W:])
        sg = _dot(hb, w_in_ref[:, C_UGATE:C_UGATE + SSM_W])
        t = (yb * _silu(sg)).astype(BF16)
        br = _dot(t, w_ssm_out_ref[...])
        g = _sigmoid(_dot(hb, w_in_ref[:, C_MERGE + D_MODEL:C_MERGE + 2 * D_MODEL]))
        out_ref[rows(rc), :] += g * br

    q_g = cvec_ref[4:5, :]
    k_g = cvec_ref[5:6, 0:KV_W]
    lane = lax.broadcasted_iota(jnp.int32, (ROW_CHUNK, LANES), 1)
    first_half = (lane & (2 * ROPE_PAIRS - 1)) < ROPE_PAIRS

    def rope(xs, cos, sin):
        swapped = jnp.where(first_half, pltpu.roll(xs, LANES - ROPE_PAIRS, 1), pltpu.roll(xs, ROPE_PAIRS, 1))
        return xs * cos + swapped * sin

    def head_ms(x2, n):
        hi = x2.astype(BF16)
        lo = (x2 - hi.astype(F32)).astype(BF16)
        ones = ones_ref[0:n, 0:n]
        return (_dot(hi, ones) + _dot(lo, ones)) * (1.0 / HEAD_DIM)

    for rc in range(n_rc):
        zq = _dot(h_scr[rows(rc), :], w_in_ref[:, C_QKV:C_QKV + ATTN_W + 2 * KV_W])
        qf, kf, vf = zq[:, :ATTN_W], zq[:, ATTN_W:ATTN_W + KV_W], zq[:, ATTN_W + KV_W:]
        qn = qf * lax.rsqrt(head_ms(qf * qf, ATTN_W) + EPS) * q_g
        kn = kf * lax.rsqrt(head_ms(kf * kf, KV_W) + EPS) * k_g
        r0 = rc * ROW_CHUNK
        sq = r0 // S
        t0 = r0 - sq * S
        if latent:
            cos = cos_ref[t0:t0 + ROW_CHUNK, :]
            sin = sin_ref[t0:t0 + ROW_CHUNK, :]
            qn = jnp.concatenate(
                [rope(qn[:, i * LANES:(i + 1) * LANES], cos, sin) for i in range(ATTN_W // LANES)], axis=1)
            kn = rope(kn, cos, sin)
        else:
            k_out_ref[sq, t0:t0 + ROW_CHUNK, :] = kn
            v_out_ref[sq, t0:t0 + ROW_CHUNK, :] = vf
        big_scr[rows(rc), 0:ATTN_W] = qn * (HEAD_DIM ** -0.5)
        kall_scr[sq, t0:t0 + ROW_CHUNK, :] = kn
        vall_scr[sq, t0:t0 + ROW_CHUNK, :] = vf
    if latent:
        kall_scr[0, S:lk, :] = ck_ref[0]
        vall_scr[0, S:lk, :] = cvv_ref[0]

    def attend(sq, row0):
        for hd in range(N_HEADS):
            kv = hd // GQ
            qh = big_scr[pl.ds(row0, ROW_CHUNK), hd * HEAD_DIM:(hd + 1) * HEAD_DIM].astype(BF16)
            kh = kall_scr[sq, :, kv * HEAD_DIM:(kv + 1) * HEAD_DIM].astype(BF16)
            vh = vall_scr[sq, :, kv * HEAD_DIM:(kv + 1) * HEAD_DIM].astype(BF16)
            s = lax.dot_general(qh, kh, (((1,), (1,)), ((), ())), preferred_element_type=F32)
            e = jnp.exp(s - jnp.max(s, axis=-1, keepdims=True))
            o = _dot(e.astype(BF16), vh) / jnp.sum(e, axis=-1, keepdims=True)
            big_scr[pl.ds(row0, ROW_CHUNK), ATTN_W + hd * HEAD_DIM:ATTN_W + (hd + 1) * HEAD_DIM] = o

    if latent:
        def attend_chunk(qc, carry):
            attend(0, pl.multiple_of(qc * ROW_CHUNK, ROW_CHUNK))
            return carry
        lax.fori_loop(0, n_rc, attend_chunk, 0)
    else:
        for sq in range(nseq):
            for qc in range(S // ROW_CHUNK):
                attend(sq, sq * S + qc * ROW_CHUNK)

    for rc in range(n_rc):
        hb = h_scr[rows(rc), :]
        ag = _dot(hb, w_in_ref[:, C_AGATE:C_AGATE + ATTN_W])
        t = (big_scr[rows(rc), ATTN_W:2 * ATTN_W] * _silu(ag)).astype(BF16)
        br = _dot(t, w_attn_out_ref[...])
        g = _sigmoid(_dot(hb, w_in_ref[:, C_MERGE + 2 * D_MODEL:C_MERGE + 3 * D_MODEL]))
        mixed = out_ref[rows(rc), :] + g * br
        out_ref[rows(rc), :] = x_ref[rows(rc), :] + gate * _dot(mixed.astype(BF16), w_out_ref[...])


def _const_spec(shape):
    nd = len(shape)
    return pl.BlockSpec(shape, lambda i, _nd=nd: (0,) * _nd, pipeline_mode=pl.Buffered(1))


def _layer_spec(shape, layer):
    nd = len(shape) - 1
    return pl.BlockSpec((None,) + tuple(shape[1:]), lambda i, _nd=nd: (layer,) + (0,) * _nd,
                        pipeline_mode=pl.Buffered(1))


def _layer_call(cfg, layer, x2d, mods, lw, extra):
    R, S, nseq, latent, past = cfg["R"], cfg["S"], cfg["nseq"], cfg["latent"], cfg["past"]
    n_rows = x2d.shape[0]
    n_blocks = n_rows // R
    stacked = [lw["ng"], lw["w_in"], lw["cw"], lw["cvec"], lw["w_conv_out"], lw["bm"], lw["cm"],
               lw["sp_lat" if latent else "sp_ctx"], lw["w_glu"], lw["w_ssm_out"]]
    stacked_tail = [lw["w_attn_out"], lw["w_out"]]
    if latent:
        mods_spec = pl.BlockSpec((None, 1, SUBLANES, D_MODEL), lambda i: (layer, i + 1, 0, 0))
    else:
        mods_spec = pl.BlockSpec((None, 1, SUBLANES, D_MODEL), lambda i: (layer, 0, 0, 0))
    row_mode = dict(pipeline_mode=pl.Buffered(1)) if latent else {}
    in_specs = [pl.BlockSpec((R, D_MODEL), lambda i: (i, 0), **row_mode), mods_spec]
    in_specs += [_layer_spec(w.shape, layer) for w in stacked]
    in_specs += [_const_spec(lw["ones"].shape)]
    in_specs += [_layer_spec(w.shape, layer) for w in stacked_tail]
    args = [x2d, mods] + stacked + [lw["ones"]] + stacked_tail
    out_shape = [jax.ShapeDtypeStruct((n_rows, D_MODEL), F32)]
    out_specs = [pl.BlockSpec((R, D_MODEL), lambda i: (i, 0), **row_mode)]
    if latent:
        cos, sin, ck, cv, h0re, h0im = extra
        in_specs += [_const_spec(cos.shape), _const_spec(sin.shape),
                     pl.BlockSpec((1, None, past, KV_W), lambda i: (i, layer, 0, 0)),
                     pl.BlockSpec((1, None, past, KV_W), lambda i: (i, layer, 0, 0)),
                     pl.BlockSpec((1, None, 2, N_CHUNK, CHUNK_STATES), lambda i: (i, layer, 0, 0, 0)),
                     pl.BlockSpec((1, None, 2, N_CHUNK, CHUNK_STATES), lambda i: (i, layer, 0, 0, 0))]
        args += [cos, sin, ck, cv, h0re, h0im]
    aliases = {}
    carried = ()
    if not latent:
        n_seq_total = n_rows // S
        depth = lw["w_in"].shape[0]
        side_shapes = [(n_seq_total, depth, S, KV_W)] * 2 + [(n_seq_total, depth, 2, SSM_G * SSM_P)] * 2
        out_shape += [jax.ShapeDtypeStruct(s, F32) for s in side_shapes]
        out_specs += [pl.BlockSpec((nseq, None) + s[2:], lambda i: (i, layer, 0, 0)) for s in side_shapes]
        carried = tuple(extra) if extra is not None else ()
        aliases = {len(args) + j: 1 + j for j in range(len(carried))}
        in_specs += [pl.BlockSpec(memory_space=pl.ANY) for _ in carried]
        args += list(carried)
    cfg = dict(cfg, n_carried=len(carried))
    scratch = [
        pltpu.VMEM((R, D_MODEL), BF16),
        pltpu.VMEM((CONV_W // LANES, nseq * (S + 2 * CONV_HALO), LANES), F32),
        pltpu.VMEM((SSM_W // LANES, N_SEG * (cfg["L"] + SEG_PAD), LANES), F32),
        pltpu.VMEM((SSM_W // LANES, R, LANES), F32),
        pltpu.VMEM((SSM_W // LANES, R, LANES), F32),
        pltpu.VMEM((2 if cfg["static_loops"] else 1, R, SCAN_COLS), F32),
        pltpu.VMEM((R, SCAN_COLS), BF16),
        pltpu.VMEM((nseq, S + past, KV_W), F32),
        pltpu.VMEM((nseq, S + past, KV_W), F32),
    ]
    return pl.pallas_call(
        functools.partial(_layer_kernel, cfg),
        grid=(n_blocks,),
        in_specs=in_specs,
        out_specs=out_specs,
        out_shape=out_shape,
        scratch_shapes=scratch,
        input_output_aliases=aliases,
        compiler_params=pltpu.CompilerParams(
            dimension_semantics=("arbitrary",),
            vmem_limit_bytes=cfg["vmem_bytes"]),
        name="latent_layer" if latent else "context_layer",
    )(*args)


def _ssm_pack(a_re, a_im, log_dt, b_re, b_im, c_re, c_im, seg_len):
    depth = a_re.shape[0]
    lam = lax.complex(a_re, a_im)
    dt = jnp.exp(log_dt)[..., None]
    lam_bar = jnp.exp(lam * dt)
    b_bar = ((lam_bar - 1.0) / lam)[..., None] * lax.complex(b_re, b_im)
    eye = jnp.eye(SCAN_GROUPS, dtype=F32)
    bb = jnp.stack([b_bar.real, b_bar.imag], axis=2)
    bb = bb.reshape(depth, 2, 2, N_CHUNK, SCAN_GROUPS, SSM_P, SSM_GC)
    tb = jnp.einsum("ldacgpk,hg->lchkdagp", bb, eye).reshape(depth, N_CHUNK, SCAN_GROUPS * SSM_GC, SCAN_COLS)
    zb = jnp.zeros_like(tb)
    odd = (jnp.arange(N_CHUNK) % 2 == 1)[None, :, None, None]
    bm = jnp.where(odd, jnp.concatenate([zb, tb], axis=2), jnp.concatenate([tb, zb], axis=2))
    cc = jnp.stack([c_re, -c_im], axis=2)
    cc = cc.reshape(depth, 2, 2, N_CHUNK, SCAN_GROUPS, SSM_GC, SSM_P)
    tc = jnp.einsum("ldacgkp,hg->lcdagphk", cc, eye).reshape(depth, N_CHUNK, SCAN_COLS, SCAN_GROUPS * SSM_GC)
    zc = jnp.zeros_like(tc)
    cm = jnp.where(odd, jnp.concatenate([zc, tc], axis=3), jnp.concatenate([tc, zc], axis=3))

    def chunk_rows(v):
        v = v.reshape(depth, 2, N_CHUNK, CHUNK_STATES).transpose(0, 2, 1, 3)
        return v.reshape(depth, N_CHUNK, 2 * CHUNK_STATES)

    def sp_for(seg):
        pows = [lam_bar] + [jnp.exp(lam * dt * float(seg * d)) for d in (1, 2, 4)]
        rows_ = []
        for p in pows:
            rows_ += [chunk_rows(p.real), chunk_rows(p.imag)]
        return jnp.stack(rows_, axis=2)

    return bm.astype(BF16), cm.astype(BF16), [sp_for(s) for s in seg_len]


def _rope_tables(n_tok):
    rows = n_tok // GRID_W
    row = jnp.repeat(jnp.arange(rows, dtype=F32), GRID_W)
    col = jnp.tile(jnp.arange(GRID_W, dtype=F32), rows)
    inv = ROPE_BASE ** (-jnp.arange(ROPE_PAIRS, dtype=F32) / ROPE_PAIRS)
    ar, ac = row[:, None] * inv[None, :], col[:, None] * inv[None, :]
    ang = jnp.concatenate([ar, ar, ac, ac], axis=-1)
    sign = jnp.tile(jnp.concatenate([-jnp.ones(ROPE_PAIRS, F32), jnp.ones(ROPE_PAIRS, F32)]), 2)
    reps = LANES // HEAD_DIM
    return jnp.tile(jnp.cos(ang), (1, reps)), jnp.tile(jnp.sin(ang) * sign, (1, reps))


def kernel(x_prompt, x_sample, cache_k, cache_v, state_ssm_re, state_ssm_im, c, c_ctx, norm_g, w_ada, b_ada, w_in, conv_dw_w, conv_dw_b, conv_ln_g, conv_ln_b, w_conv_out, ssm_a_re, ssm_a_im, ssm_log_dt, ssm_b_re, ssm_b_im, ssm_c_re, ssm_c_im, ssm_d, w_ssm_glu, w_ssm_out, q_norm_g, k_norm_g, w_attn_out, w_out):
    batch, seq, d_model = x_prompt.shape
    dec_batch, dec_seq, _ = x_sample.shape
    depth = w_in.shape[0]
    past = cache_k.shape[2]
    assert d_model == D_MODEL and w_in.shape[2] == IN_COLS and dec_seq % GRID_W == 0

    ctx_cfg = dict(R=2 * seq, S=seq, nseq=2, latent=False, past=0, static_loops=True,
                   vmem_bytes=52 * 1024 * 1024)
    lat_cfg = dict(R=dec_seq, S=dec_seq, nseq=1, latent=True, past=past, static_loops=False,
                   vmem_bytes=60 * 1024 * 1024)
    for cfg in (ctx_cfg, lat_cfg):
        cfg["L"] = cfg["R"] // N_SEG
        cfg["q"] = cfg["S"] // cfg["L"]
        assert cfg["R"] % ROW_CHUNK == 0 and cfg["S"] % ROW_CHUNK == 0 and cfg["q"] in (1, 2, 4, 8)
        assert cfg["vmem_bytes"] <= VMEM_PHYSICAL_BYTES
        assert ROW_CHUNK % cfg["L"] == 0 and ((cfg["L"] + SEG_PAD) // SUBLANES) % 2 == 1

    cond = jnp.zeros((SUBLANES, d_model), F32).at[0].set(c_ctx).at[1:1 + dec_batch].set(c)
    mods_all = _adaln(cond, w_ada, b_ada)
    cos, sin = _rope_tables(dec_seq)
    head_id = jnp.arange(ATTN_W) // HEAD_DIM
    ones = (head_id[:, None] == head_id[None, :]).astype(BF16)

    bm, cm, (sp_ctx, sp_lat) = _ssm_pack(ssm_a_re, ssm_a_im, ssm_log_dt, ssm_b_re, ssm_b_im, ssm_c_re, ssm_c_im,
                                         (ctx_cfg["L"], lat_cfg["L"]))
    zrow = jnp.zeros((depth, CONV_W), F32)
    cvec = jnp.stack([conv_dw_b, conv_ln_g, conv_ln_b, ssm_d, jnp.tile(q_norm_g, (1, N_HEADS)),
                      jnp.tile(k_norm_g, (1, N_HEADS)), zrow, zrow], axis=1)
    lw = dict(
        ng=norm_g.reshape(depth, 1, d_model), w_in=w_in.astype(BF16),
        cw=jnp.concatenate([conv_dw_w, zrow[:, None, :]], axis=1), cvec=cvec,
        w_conv_out=w_conv_out.astype(BF16), bm=bm, cm=cm, sp_ctx=sp_ctx, sp_lat=sp_lat,
        w_glu=w_ssm_glu.astype(BF16), w_ssm_out=w_ssm_out.astype(BF16), ones=ones,
        w_attn_out=w_attn_out.astype(BF16), w_out=w_out.astype(BF16))
    m3 = mods_all[:, :1 + dec_batch].reshape(depth, 1 + dec_batch, 3, d_model)
    mods = jnp.concatenate(
        [m3, jnp.zeros((depth, 1 + dec_batch, SUBLANES - 3, d_model), F32)], axis=2)
    ck_all = cache_k.reshape(dec_batch, depth, past, KV_W)
    cv_all = cache_v.reshape(dec_batch, depth, past, KV_W)
    h0re = state_ssm_re.reshape(dec_batch, depth, 2, N_CHUNK, CHUNK_STATES)
    h0im = state_ssm_im.reshape(dec_batch, depth, 2, N_CHUNK, CHUNK_STATES)

    yp = x_prompt.reshape(batch * seq, d_model)
    ys = x_sample.reshape(dec_batch * dec_seq, d_model)
    side = None
    for l in range(depth):
        yp, *side = _layer_call(ctx_cfg, l, yp, mods, lw, side)
        (ys,) = _layer_call(lat_cfg, l, ys, mods, lw, (cos, sin, ck_all, cv_all, h0re, h0im))
    k_new, v_new, st_re, st_im = side

    return (yp.reshape(batch, seq, d_model), ys.reshape(dec_batch, dec_seq, d_model),
            k_new.reshape(batch, depth, seq, N_KV_HEADS, HEAD_DIM),
            v_new.reshape(batch, depth, seq, N_KV_HEADS, HEAD_DIM),
            st_re.reshape(batch, depth, 2, SSM_G, SSM_P), st_im.reshape(batch, depth, 2, SSM_G, SSM_P))
```

```python
import functools
import math

import jax
import jax.numpy as jnp
import numpy as np
from jax import lax
from jax.experimental import pallas as pl
from jax.experimental.pallas import tpu as pltpu

F32 = jnp.float32
BF16 = jnp.bfloat16

D_MODEL = 1024
CONV_W = 512
CONV_K = 31
SSM_W = 512
SSM_GC = 16
SSM_G = 32
SSM_P = 64
N_HEADS = 8
N_KV_HEADS = 2
HEAD_DIM = 64
GQ = N_HEADS // N_KV_HEADS
ATTN_W = N_HEADS * HEAD_DIM
KV_W = N_KV_HEADS * HEAD_DIM
ROPE_PAIRS = HEAD_DIM // 4
ROPE_BASE = 10000.0
GRID_W = 64
EPS = 1e-6

C_CONV = 0
C_CGATE = 1024
C_U = 1536
C_UGATE = 2048
C_QKV = 2560
C_AGATE = 3328
C_MERGE = 3840
IN_COLS = 6912

SUBLANES = 8
LANES = 128
N_SEG = SUBLANES
SEG_PAD = 8
SCAN_GROUPS = 4
N_CHUNK = SSM_G // SCAN_GROUPS
CHUNK_STATES = SCAN_GROUPS * SSM_P
SCAN_COLS = 4 * CHUNK_STATES
CONV_TILE = 64
CONV_HALO = 16
ROW_CHUNK = 256
VMEM_PHYSICAL_BYTES = 64 * 1024 * 1024


def _sigmoid(x):
    return jax.nn.sigmoid(x)


def _silu(x):
    return x * jax.nn.sigmoid(x)


def _gelu_tanh(x):
    return 0.5 * x * (1.0 + jnp.tanh(math.sqrt(2.0 / math.pi) * (x + 0.044715 * (x * x * x))))


def _dot(a, b):
    return jnp.dot(a, b, preferred_element_type=F32)


def _aligned(x, m):
    return x if isinstance(x, int) else pl.multiple_of(x, m)


def _for(n, body, carry, *, static, unroll=1):
    if static:
        for i in range(n):
            carry = body(i, carry)
        return carry
    return lax.fori_loop(0, n, body, carry, unroll=unroll)


def _adaln_kernel(cond_ref, w_ref, b_ref, o_ref):
    s = _silu(cond_ref[...])
    o_ref[0] = jnp.dot(s, w_ref[0], preferred_element_type=F32,
                       precision=lax.Precision.HIGHEST) + b_ref[0]


def _adaln(cond, w_ada, b_ada):
    depth, d, n = w_ada.shape
    tn = 512
    return pl.pallas_call(
        _adaln_kernel,
        grid=(depth, n // tn),
        in_specs=[
            pl.BlockSpec((SUBLANES, d), lambda l, j: (0, 0)),
            pl.BlockSpec((1, d, tn), lambda l, j: (l, 0, j)),
            pl.BlockSpec((1, 1, tn), lambda l, j: (l, 0, j)),
        ],
        out_specs=pl.BlockSpec((1, SUBLANES, tn), lambda l, j: (l, 0, j)),
        out_shape=jax.ShapeDtypeStruct((depth, SUBLANES, n), F32),
        name="adaln",
    )(cond, w_ada, b_ada.reshape(depth, 1, n))


def _layer_kernel(cfg, *refs):
    R, S, L, q, nseq, latent, past = (cfg[k] for k in ("R", "S", "L", "q", "nseq", "latent", "past"))
    lk = S + past
    n_rc = R // ROW_CHUNK
    it = iter(refs)
    x_ref, mods_ref, ng_ref, w_in_ref, cw_ref, cvec_ref = (next(it) for _ in range(6))
    w_conv_out_ref, bm_ref, cm_ref, sp_ref, w_glu_ref, w_ssm_out_ref = (next(it) for _ in range(6))
    ones_ref, w_attn_out_ref, w_out_ref = (next(it) for _ in range(3))
    if latent:
        cos_ref, sin_ref, ck_ref, cvv_ref, h0re_ref, h0im_ref = (next(it) for _ in range(6))
    for _ in range(cfg["n_carried"]):
        next(it)
    out_ref = next(it)
    if not latent:
        k_out_ref, v_out_ref, st_re_ref, st_im_ref = (next(it) for _ in range(4))
    h_scr, ypad_scr, u_scr, up_scr, yp_scr, bigs_scr, hs_scr, kall_scr, vall_scr = (next(it) for _ in range(9))
    static_loops = cfg["static_loops"]
    bigs = [bigs_scr.at[i] for i in range(bigs_scr.shape[0])]
    big_scr = bigs[0]
    n_slab = SSM_W // LANES
    seg_pitch = L + SEG_PAD

    ng = ng_ref[...]
    shift = mods_ref[0, 0:1, :]
    scale1 = 1.0 + mods_ref[0, 1:2, :]
    gate = mods_ref[0, 2:3, :]
    pitch = S + 2 * CONV_HALO

    def rows(rc):
        return slice(rc * ROW_CHUNK, (rc + 1) * ROW_CHUNK)

    zero_halo = jnp.zeros((CONV_HALO, LANES), F32)
    for sq in range(nseq):
        for s4 in range(n_slab):
            ypad_scr[s4, sq * pitch:sq * pitch + CONV_HALO, :] = zero_halo
            ypad_scr[s4, sq * pitch + CONV_HALO + S:(sq + 1) * pitch, :] = zero_halo
    for rc in range(n_rc):
        x = x_ref[rows(rc), :]
        ms = jnp.mean(x * x, axis=-1, keepdims=True)
        hn = (x * lax.rsqrt(ms + EPS) * ng) * scale1 + shift
        hb = hn.astype(BF16)
        h_scr[rows(rc), :] = hb
        zc = _dot(hb, w_in_ref[:, C_CONV:C_CONV + 2 * CONV_W])
        y = zc[:, :CONV_W] * _sigmoid(zc[:, CONV_W:])
        r0 = rc * ROW_CHUNK
        sq = r0 // S
        p0 = sq * pitch + CONV_HALO + (r0 - sq * S)
        for s4 in range(n_slab):
            ypad_scr[s4, p0:p0 + ROW_CHUNK, :] = y[:, s4 * LANES:(s4 + 1) * LANES]

    conv_b = cvec_ref[0:1, :]
    ln_g = cvec_ref[1:2, :]
    ln_b = cvec_ref[2:3, :]
    log2_s = int(math.log2(S))

    def conv_tile(t, carry):
        r0 = _aligned(t * CONV_TILE, CONV_TILE)
        sq = (t * CONV_TILE) >> log2_s
        w0 = r0 + sq * (2 * CONV_HALO)
        for s4 in range(n_slab):
            lanes = slice(s4 * LANES, (s4 + 1) * LANES)
            a = jnp.broadcast_to(conv_b[:, lanes], (CONV_TILE, LANES))
            for k in range(CONV_K):
                a = a + cw_ref[k:k + 1, lanes] * ypad_scr[s4, pl.ds(w0 + 1 + k, CONV_TILE, stride=1), :]
            big_scr[pl.ds(r0, CONV_TILE), lanes] = a
        return carry

    _for(R // CONV_TILE, conv_tile, 0, static=static_loops)

    for rc in range(n_rc):
        hb = h_scr[rows(rc), :]
        cg = _dot(hb, w_in_ref[:, C_CGATE:C_CGATE + CONV_W])
        acc = big_scr[rows(rc), 0:CONV_W]
        mu = jnp.mean(acc, axis=-1, keepdims=True)
        xc = acc - mu
        var = jnp.mean(xc * xc, axis=-1, keepdims=True)
        yl = xc * lax.rsqrt(var + EPS) * ln_g + ln_b
        t = (_silu(yl) * _silu(cg)).astype(BF16)
        br = _dot(t, w_conv_out_ref[...])
        g = _sigmoid(_dot(hb, w_in_ref[:, C_MERGE:C_MERGE + D_MODEL]))
        out_ref[rows(rc), :] = g * br

    segs_per_chunk = ROW_CHUNK // L

    def seg_rows(rc, j):
        k = rc * segs_per_chunk + j
        return slice(k * seg_pitch, k * seg_pitch + L)

    for rc in range(n_rc):
        u = _dot(h_scr[rows(rc), :], w_in_ref[:, C_U:C_U + SSM_W])
        for j in range(segs_per_chunk):
            for s4 in range(n_slab):
                u_scr[s4, seg_rows(rc, j), :] = u[j * L:(j + 1) * L, s4 * LANES:(s4 + 1) * LANES]

    def permute_rows(i, carry):
        dst = _aligned(i * N_SEG, N_SEG)
        for s4 in range(n_slab):
            up_scr[s4, pl.ds(dst, N_SEG), :] = u_scr[s4, pl.ds(i, N_SEG, stride=seg_pitch), :]
        return carry

    _for(L, permute_rows, 0, static=static_loops)

    tile = (N_SEG, CHUNK_STATES)
    seg_in_seq = lax.broadcasted_iota(jnp.int32, tile, 0) & (q - 1)
    c0, c1, c2, c3, c4 = (i * CHUNK_STATES for i in range(5))

    def cmul(ar, ai, br_, bi_):
        return ar * br_ - ai * bi_, ar * bi_ + ai * br_

    def input_map(c):
        for rc in range(n_rc):
            bigs[c % len(bigs)][rows(rc), :] = _dot(up_scr[c // 2, rows(rc), :].astype(BF16), bm_ref[c])

    lookahead = len(bigs) - 1
    for c in range(lookahead):
        input_map(c)
    for c in range(N_CHUNK):
        slab, half = divmod(c, 2)
        bu_scr = bigs[c % len(bigs)]
        if c + lookahead < N_CHUNK:
            input_map(c + lookahead)

        def sp_row(i, d):
            return jnp.broadcast_to(sp_ref[c, i:i + 1, d * CHUNK_STATES:(d + 1) * CHUNK_STATES], tile)

        lfr, lfi, lbr, lbi = sp_row(0, 0), sp_row(1, 0), sp_row(0, 1), sp_row(1, 1)

        def fwd_step(hr, hi, row):
            row = _aligned(row, N_SEG)
            return (lfr * hr - lfi * hi + bu_scr[pl.ds(row, N_SEG), c0:c1],
                    lfr * hi + lfi * hr + bu_scr[pl.ds(row, N_SEG), c1:c2])

        def bwd_step(hr, hi, row):
            row = _aligned(row, N_SEG)
            return (lbr * hr - lbi * hi + bu_scr[pl.ds(row, N_SEG), c2:c3],
                    lbr * hi + lbi * hr + bu_scr[pl.ds(row, N_SEG), c3:c4])

        def ends_step(i, carry):
            hfr, hfi, hbr, hbi = carry
            hfr, hfi = fwd_step(hfr, hfi, i * N_SEG)
            hbr, hbi = bwd_step(hbr, hbi, (L - 1 - i) * N_SEG)
            return hfr, hfi, hbr, hbi

        z = jnp.zeros(tile, F32)
        efr, efi, ebr, ebi = _for(L, ends_step, (z, z, z, z), static=static_loops, unroll=8)

        if latent:
            h0 = [jnp.broadcast_to(r[0, d, c:c + 1, :], tile) for d in range(2) for r in (h0re_ref, h0im_ref)]
        else:
            h0 = [z, z, z, z]
        hfr = jnp.where(seg_in_seq == 0, h0[0], pltpu.roll(efr, 1, 0))
        hfi = jnp.where(seg_in_seq == 0, h0[1], pltpu.roll(efi, 1, 0))
        hbr = jnp.where(seg_in_seq == q - 1, h0[2], pltpu.roll(ebr, N_SEG - 1, 0))
        hbi = jnp.where(seg_in_seq == q - 1, h0[3], pltpu.roll(ebi, N_SEG - 1, 0))
        for j, dist in enumerate((1, 2, 4)):
            if dist >= q:
                break
            pfr, pfi, pbr, pbi = sp_row(2 + 2 * j, 0), sp_row(3 + 2 * j, 0), sp_row(2 + 2 * j, 1), sp_row(3 + 2 * j, 1)
            ar, ai = cmul(pfr, pfi, pltpu.roll(hfr, dist, 0), pltpu.roll(hfi, dist, 0))
            keep = seg_in_seq >= dist
            hfr, hfi = hfr + jnp.where(keep, ar, 0.0), hfi + jnp.where(keep, ai, 0.0)
            ar, ai = cmul(pbr, pbi, pltpu.roll(hbr, N_SEG - dist, 0), pltpu.roll(hbi, N_SEG - dist, 0))
            keep = seg_in_seq <= q - 1 - dist
            hbr, hbi = hbr + jnp.where(keep, ar, 0.0), hbi + jnp.where(keep, ai, 0.0)

        if not latent:
            plr, pli = sp_row(2, 0), sp_row(3, 0)
            ffr, ffi = cmul(plr, pli, hfr, hfi)
            ffr, ffi = ffr + efr, ffi + efi
            plr, pli = sp_row(2, 1), sp_row(3, 1)
            fbr, fbi = cmul(plr, pli, hbr, hbi)
            fbr, fbi = fbr + ebr, fbi + ebi
            cols = slice(c * CHUNK_STATES, (c + 1) * CHUNK_STATES)
            for sq in range(nseq):
                lf = (sq + 1) * q - 1
                lb = sq * q
                st_re_ref[sq, 0:1, cols] = ffr[lf:lf + 1, :]
                st_im_ref[sq, 0:1, cols] = ffi[lf:lf + 1, :]
                st_re_ref[sq, 1:2, cols] = fbr[lb:lb + 1, :]
                st_im_ref[sq, 1:2, cols] = fbi[lb:lb + 1, :]

        def scan_pair(j, carry):
            hfr, hfi, hbr, hbi = carry
            rf = _aligned(j * (2 * N_SEG), 2 * N_SEG)
            rb = _aligned((L - 2 - 2 * j) * N_SEG, 2 * N_SEG)
            f1r, f1i = fwd_step(hfr, hfi, rf)
            f2r, f2i = fwd_step(f1r, f1i, rf + N_SEG)
            b1r, b1i = bwd_step(hbr, hbi, rb + N_SEG)
            b2r, b2i = bwd_step(b1r, b1i, rb)
            hs_scr[pl.ds(rf, 2 * N_SEG), c0:c1] = jnp.concatenate([f1r, f2r], axis=0).astype(BF16)
            hs_scr[pl.ds(rf, 2 * N_SEG), c1:c2] = jnp.concatenate([f1i, f2i], axis=0).astype(BF16)
            hs_scr[pl.ds(rb, 2 * N_SEG), c2:c3] = jnp.concatenate([b2r, b1r], axis=0).astype(BF16)
            hs_scr[pl.ds(rb, 2 * N_SEG), c3:c4] = jnp.concatenate([b2i, b1i], axis=0).astype(BF16)
            return f2r, f2i, b2r, b2i

        _for(L // 2, scan_pair, (hfr, hfi, hbr, hbi), static=static_loops, unroll=2)

        for rc in range(n_rc):
            yc = _dot(hs_scr[rows(rc), :], cm_ref[c])
            if half == 0:
                yp_scr[slab, rows(rc), :] = yc
            else:
                yp_scr[slab, rows(rc), :] += yc

    def unpermute_rows(j, carry):
        src = j * (N_SEG * SUBLANES)
        for k in range(N_SEG):
            dst = _aligned(k * L + j * SUBLANES, SUBLANES)
            for s4 in range(n_slab):
                big_scr[pl.ds(dst, SUBLANES), s4 * LANES:(s4 + 1) * LANES] = (
                    yp_scr[s4, pl.ds(src + k, SUBLANES, stride=N_SEG), :])
        return carry

    _for(L // SUBLANES, unpermute_rows, 0, static=static_loops)

    ssm_d = cvec_ref[3:4, :]
    for rc in range(n_rc):
        hb = h_scr[rows(rc), :]
        u = jnp.concatenate(
            [jnp.concatenate([u_scr[s4, seg_rows(rc, j), :] for j in range(segs_per_chunk)], axis=0)
             for s4 in range(n_slab)], axis=1)
        yb = big_scr[rows(rc), 0:SSM_W] + ssm_d * u
        z2 = _dot(_gelu_tanh(yb).astype(BF16), w_glu_ref[...])
        yb = z2[:, :SSM_W] * _sigmoid(z2[:, SSM_W:])
        sg = _dot(hb, w_in_ref[:, C_UGATE:C_UGATE + SSM_W])
        t = (yb * _silu(sg)).astype(BF16)
        br = _dot(t, w_ssm_out_ref[...])
        g = _sigmoid(_dot(hb, w_in_ref[:, C_MERGE + D_MODEL:C_MERGE + 2 * D_MODEL]))
        out_ref[rows(rc), :] += g * br

    q_g = cvec_ref[4:5, :]
    k_g = cvec_ref[5:6, 0:KV_W]
    lane = lax.broadcasted_iota(jnp.int32, (ROW_CHUNK, LANES), 1)
    first_half = (lane & (2 * ROPE_PAIRS - 1)) < ROPE_PAIRS

    def rope(xs, cos, sin):
        swapped = jnp.where(first_half, pltpu.roll(xs, LANES - ROPE_PAIRS, 1), pltpu.roll(xs, ROPE_PAIRS, 1))
        return xs * cos + swapped * sin

    def head_ms(x2, n):
        hi = x2.astype(BF16)
        lo = (x2 - hi.astype(F32)).astype(BF16)
        ones = ones_ref[0:n, 0:n]
        return (_dot(hi, ones) + _dot(lo, ones)) * (1.0 / HEAD_DIM)

    for rc in range(n_rc):
        zq = _dot(h_scr[rows(rc), :], w_in_ref[:, C_QKV:C_QKV + ATTN_W + 2 * KV_W])
        qf, kf, vf = zq[:, :ATTN_W], zq[:, ATTN_W:ATTN_W + KV_W], zq[:, ATTN_W + KV_W:]
        qn = qf * lax.rsqrt(head_ms(qf * qf, ATTN_W) + EPS) * q_g
        kn = kf * lax.rsqrt(head_ms(kf * kf, KV_W) + EPS) * k_g
        r0 = rc * ROW_CHUNK
        sq = r0 // S
        t0 = r0 - sq * S
        if latent:
            cos = cos_ref[t0:t0 + ROW_CHUNK, :]
            sin = sin_ref[t0:t0 + ROW_CHUNK, :]
            qn = jnp.concatenate(
                [rope(qn[:, i * LANES:(i + 1) * LANES], cos, sin) for i in range(ATTN_W // LANES)], axis=1)
            kn = rope(kn, cos, sin)
        else:
            k_out_ref[sq, t0:t0 + ROW_CHUNK, :] = kn
            v_out_ref[sq, t0:t0 + ROW_CHUNK, :] = vf
        big_scr[rows(rc), 0:ATTN_W] = qn * (HEAD_DIM ** -0.5)
        kall_scr[sq, t0:t0 + ROW_CHUNK, :] = kn
        vall_scr[sq, t0:t0 + ROW_CHUNK, :] = vf
    if latent:
        kall_scr[0, S:lk, :] = ck_ref[0]
        vall_scr[0, S:lk, :] = cvv_ref[0]

    def attend(sq, row0):
        for hd in range(N_HEADS):
            kv = hd // GQ
            qh = big_scr[pl.ds(row0, ROW_CHUNK), hd * HEAD_DIM:(hd + 1) * HEAD_DIM].astype(BF16)
            kh = kall_scr[sq, :, kv * HEAD_DIM:(kv + 1) * HEAD_DIM].astype(BF16)
            vh = vall_scr[sq, :, kv * HEAD_DIM:(kv + 1) * HEAD_DIM].astype(BF16)
            s = lax.dot_general(qh, kh, (((1,), (1,)), ((), ())), preferred_element_type=F32)
            e = jnp.exp(s - jnp.max(s, axis=-1, keepdims=True))
            o = _dot(e.astype(BF16), vh) / jnp.sum(e, axis=-1, keepdims=True)
            big_scr[pl.ds(row0, ROW_CHUNK), ATTN_W + hd * HEAD_DIM:ATTN_W + (hd + 1) * HEAD_DIM] = o

    if latent:
        def attend_chunk(qc, carry):
            attend(0, pl.multiple_of(qc * ROW_CHUNK, ROW_CHUNK))
            return carry
        lax.fori_loop(0, n_rc, attend_chunk, 0)
    else:
        for sq in range(nseq):
            for qc in range(S // ROW_CHUNK):
                attend(sq, sq * S + qc * ROW_CHUNK)

    for rc in range(n_rc):
        hb = h_scr[rows(rc), :]
        ag = _dot(hb, w_in_ref[:, C_AGATE:C_AGATE + ATTN_W])
        t = (big_scr[rows(rc), ATTN_W:2 * ATTN_W] * _silu(ag)).astype(BF16)
        br = _dot(t, w_attn_out_ref[...])
        g = _sigmoid(_dot(hb, w_in_ref[:, C_MERGE + 2 * D_MODEL:C_MERGE + 3 * D_MODEL]))
        mixed = out_ref[rows(rc), :] + g * br
        out_ref[rows(rc), :] = x_ref[rows(rc), :] + gate * _dot(mixed.astype(BF16), w_out_ref[...])


def _const_spec(shape):
    nd = len(shape)
    return pl.BlockSpec(shape, lambda i, _nd=nd: (0,) * _nd, pipeline_mode=pl.Buffered(1))


def _layer_spec(shape, layer):
    nd = len(shape) - 1
    return pl.BlockSpec((None,) + tuple(shape[1:]), lambda i, _nd=nd: (layer,) + (0,) * _nd,
                        pipeline_mode=pl.Buffered(1))


def _layer_call(cfg, layer, x2d, mods, lw, extra):
    R, S, nseq, latent, past = cfg["R"], cfg["S"], cfg["nseq"], cfg["latent"], cfg["past"]
    n_rows = x2d.shape[0]
    n_blocks = n_rows // R
    stacked = [lw["ng"], lw["w_in"], lw["cw"], lw["cvec"], lw["w_conv_out"], lw["bm"], lw["cm"],
               lw["sp_lat" if latent else "sp_ctx"], lw["w_glu"], lw["w_ssm_out"]]
    stacked_tail = [lw["w_attn_out"], lw["w_out"]]
    if latent:
        mods_spec = pl.BlockSpec((None, 1, SUBLANES, D_MODEL), lambda i: (layer, i + 1, 0, 0))
    else:
        mods_spec = pl.BlockSpec((None, 1, SUBLANES, D_MODEL), lambda i: (layer, 0, 0, 0))
    row_mode = dict(pipeline_mode=pl.Buffered(1)) if latent else {}
    in_specs = [pl.BlockSpec((R, D_MODEL), lambda i: (i, 0), **row_mode), mods_spec]
    in_specs += [_layer_spec(w.shape, layer) for w in stacked]
    in_specs += [_const_spec(lw["ones"].shape)]
    in_specs += [_layer_spec(w.shape, layer) for w in stacked_tail]
    args = [x2d, mods] + stacked + [lw["ones"]] + stacked_tail
    out_shape = [jax.ShapeDtypeStruct((n_rows, D_MODEL), F32)]
    out_specs = [pl.BlockSpec((R, D_MODEL), lambda i: (i, 0), **row_mode)]
    if latent:
        cos, sin, ck, cv, h0re, h0im = extra
        in_specs += [_const_spec(cos.shape), _const_spec(sin.shape),
                     pl.BlockSpec((1, None, past, KV_W), lambda i: (i, layer, 0, 0)),
                     pl.BlockSpec((1, None, past, KV_W), lambda i: (i, layer, 0, 0)),
                     pl.BlockSpec((1, None, 2, N_CHUNK, CHUNK_STATES), lambda i: (i, layer, 0, 0, 0)),
                     pl.BlockSpec((1, None, 2, N_CHUNK, CHUNK_STATES), lambda i: (i, layer, 0, 0, 0))]
        args += [cos, sin, ck, cv, h0re, h0im]
    aliases = {}
    carried = ()
    if not latent:
        n_seq_total = n_rows // S
        depth = lw["w_in"].shape[0]
        side_shapes = [(n_seq_total, depth, S, KV_W)] * 2 + [(n_seq_total, depth, 2, SSM_G * SSM_P)] * 2
        out_shape += [jax.ShapeDtypeStruct(s, F32) for s in side_shapes]
        out_specs += [pl.BlockSpec((nseq, None) + s[2:], lambda i: (i, layer, 0, 0)) for s in side_shapes]
        carried = tuple(extra) if extra is not None else ()
        aliases = {len(args) + j: 1 + j for j in range(len(carried))}
        in_specs += [pl.BlockSpec(memory_space=pl.ANY) for _ in carried]
        args += list(carried)
    cfg = dict(cfg, n_carried=len(carried))
    scratch = [
        pltpu.VMEM((R, D_MODEL), BF16),
        pltpu.VMEM((CONV_W // LANES, nseq * (S + 2 * CONV_HALO), LANES), F32),
        pltpu.VMEM((SSM_W // LANES, N_SEG * (cfg["L"] + SEG_PAD), LANES), F32),
        pltpu.VMEM((SSM_W // LANES, R, LANES), F32),
        pltpu.VMEM((SSM_W // LANES, R, LANES), F32),
        pltpu.VMEM((2 if cfg["static_loops"] else 1, R, SCAN_COLS), F32),
        pltpu.VMEM((R, SCAN_COLS), BF16),
        pltpu.VMEM((nseq, S + past, KV_W), F32),
        pltpu.VMEM((nseq, S + past, KV_W), F32),
    ]
    return pl.pallas_call(
        functools.partial(_layer_kernel, cfg),
        grid=(n_blocks,),
        in_specs=in_specs,
        out_specs=out_specs,
        out_shape=out_shape,
        scratch_shapes=scratch,
        input_output_aliases=aliases,
        compiler_params=pltpu.CompilerParams(
            dimension_semantics=("arbitrary",),
            vmem_limit_bytes=cfg["vmem_bytes"]),
        name="latent_layer" if latent else "context_layer",
    )(*args)


def _ssm_pack(a_re, a_im, log_dt, b_re, b_im, c_re, c_im, seg_len):
    depth = a_re.shape[0]
    lam = lax.complex(a_re, a_im)
    dt = jnp.exp(log_dt)[..., None]
    lam_bar = jnp.exp(lam * dt)
    b_bar = ((lam_bar - 1.0) / lam)[..., None] * lax.complex(b_re, b_im)
    eye = jnp.eye(SCAN_GROUPS, dtype=F32)
    bb = jnp.stack([b_bar.real, b_bar.imag], axis=2)
    bb = bb.reshape(depth, 2, 2, N_CHUNK, SCAN_GROUPS, SSM_P, SSM_GC)
    tb = jnp.einsum("ldacgpk,hg->lchkdagp", bb, eye).reshape(depth, N_CHUNK, SCAN_GROUPS * SSM_GC, SCAN_COLS)
    zb = jnp.zeros_like(tb)
    odd = (jnp.arange(N_CHUNK) % 2 == 1)[None, :, None, None]
    bm = jnp.where(odd, jnp.concatenate([zb, tb], axis=2), jnp.concatenate([tb, zb], axis=2))
    cc = jnp.stack([c_re, -c_im], axis=2)
    cc = cc.reshape(depth, 2, 2, N_CHUNK, SCAN_GROUPS, SSM_GC, SSM_P)
    tc = jnp.einsum("ldacgkp,hg->lcdagphk", cc, eye).reshape(depth, N_CHUNK, SCAN_COLS, SCAN_GROUPS * SSM_GC)
    zc = jnp.zeros_like(tc)
    cm = jnp.where(odd, jnp.concatenate([zc, tc], axis=3), jnp.concatenate([tc, zc], axis=3))

    def chunk_rows(v):
        v = v.reshape(depth, 2, N_CHUNK, CHUNK_STATES).transpose(0, 2, 1, 3)
        return v.reshape(depth, N_CHUNK, 2 * CHUNK_STATES)

    def sp_for(seg):
        pows = [lam_bar] + [jnp.exp(lam * dt * float(seg * d)) for d in (1, 2, 4)]
        rows_ = []
        for p in pows:
            rows_ += [chunk_rows(p.real), chunk_rows(p.imag)]
        return jnp.stack(rows_, axis=2)

    return bm.astype(BF16), cm.astype(BF16), [sp_for(s) for s in seg_len]


def _rope_tables(n_tok):
    rows = n_tok // GRID_W
    row = jnp.repeat(jnp.arange(rows, dtype=F32), GRID_W)
    col = jnp.tile(jnp.arange(GRID_W, dtype=F32), rows)
    inv = ROPE_BASE ** (-jnp.arange(ROPE_PAIRS, dtype=F32) / ROPE_PAIRS)
    ar, ac = row[:, None] * inv[None, :], col[:, None] * inv[None, :]
    ang = jnp.concatenate([ar, ar, ac, ac], axis=-1)
    sign = jnp.tile(jnp.concatenate([-jnp.ones(ROPE_PAIRS, F32), jnp.ones(ROPE_PAIRS, F32)]), 2)
    reps = LANES // HEAD_DIM
    return jnp.tile(jnp.cos(ang), (1, reps)), jnp.tile(jnp.sin(ang) * sign, (1, reps))


def kernel(x_prompt, x_sample, cache_k, cache_v, state_ssm_re, state_ssm_im, c, c_ctx, norm_g, w_ada, b_ada, w_in, conv_dw_w, conv_dw_b, conv_ln_g, conv_ln_b, w_conv_out, ssm_a_re, ssm_a_im, ssm_log_dt, ssm_b_re, ssm_b_im, ssm_c_re, ssm_c_im, ssm_d, w_ssm_glu, w_ssm_out, q_norm_g, k_norm_g, w_attn_out, w_out):
    batch, seq, d_model = x_prompt.shape
    dec_batch, dec_seq, _ = x_sample.shape
    depth = w_in.shape[0]
    past = cache_k.shape[2]
    assert d_model == D_MODEL and w_in.shape[2] == IN_COLS and dec_seq % GRID_W == 0

    ctx_cfg = dict(R=2 * seq, S=seq, nseq=2, latent=False, past=0, static_loops=True,
                   vmem_bytes=52 * 1024 * 1024)
    lat_cfg = dict(R=dec_seq, S=dec_seq, nseq=1, latent=True, past=past, static_loops=False,
                   vmem_bytes=60 * 1024 * 1024)
    for cfg in (ctx_cfg, lat_cfg):
        cfg["L"] = cfg["R"] // N_SEG
        cfg["q"] = cfg["S"] // cfg["L"]
        assert cfg["R"] % ROW_CHUNK == 0 and cfg["S"] % ROW_CHUNK == 0 and cfg["q"] in (1, 2, 4, 8)
        assert cfg["vmem_bytes"] <= VMEM_PHYSICAL_BYTES
        assert ROW_CHUNK % cfg["L"] == 0 and ((cfg["L"] + SEG_PAD) // SUBLANES) % 2 == 1

    cond = jnp.zeros((SUBLANES, d_model), F32).at[0].set(c_ctx).at[1:1 + dec_batch].set(c)
    mods_all = _adaln(cond, w_ada, b_ada)
    cos, sin = _rope_tables(dec_seq)
    head_id = jnp.arange(ATTN_W) // HEAD_DIM
    ones = (head_id[:, None] == head_id[None, :]).astype(BF16)

    bm, cm, (sp_ctx, sp_lat) = _ssm_pack(ssm_a_re, ssm_a_im, ssm_log_dt, ssm_b_re, ssm_b_im, ssm_c_re, ssm_c_im,
                                         (ctx_cfg["L"], lat_cfg["L"]))
    zrow = jnp.zeros((depth, CONV_W), F32)
    cvec = jnp.stack([conv_dw_b, conv_ln_g, conv_ln_b, ssm_d, jnp.tile(q_norm_g, (1, N_HEADS)),
                      jnp.tile(k_norm_g, (1, N_HEADS)), zrow, zrow], axis=1)
    lw = dict(
        ng=norm_g.reshape(depth, 1, d_model), w_in=w_in.astype(BF16),
        cw=jnp.concatenate([conv_dw_w, zrow[:, None, :]], axis=1), cvec=cvec,
        w_conv_out=w_conv_out.astype(BF16), bm=bm, cm=cm, sp_ctx=sp_ctx, sp_lat=sp_lat,
        w_glu=w_ssm_glu.astype(BF16), w_ssm_out=w_ssm_out.astype(BF16), ones=ones,
        w_attn_out=w_attn_out.astype(BF16), w_out=w_out.astype(BF16))
    m3 = mods_all[:, :1 + dec_batch].reshape(depth, 1 + dec_batch, 3, d_model)
    mods = jnp.concatenate(
        [m3, jnp.zeros((depth, 1 + dec_batch, SUBLANES - 3, d_model), F32)], axis=2)
    ck_all = cache_k.reshape(dec_batch, depth, past, KV_W)
    cv_all = cache_v.reshape(dec_batch, depth, past, KV_W)
    h0re = state_ssm_re.reshape(dec_batch, depth, 2, N_CHUNK, CHUNK_STATES)
    h0im = state_ssm_im.reshape(dec_batch, depth, 2, N_CHUNK, CHUNK_STATES)

    yp = x_prompt.reshape(batch * seq, d_model)
    ys = x_sample.reshape(dec_batch * dec_seq, d_model)
    side = [jnp.zeros((batch, depth, seq, KV_W), F32), jnp.zeros((batch, depth, seq, KV_W), F32),
            jnp.zeros((batch, depth, 2, SSM_G * SSM_P), F32), jnp.zeros((batch, depth, 2, SSM_G * SSM_P), F32)]
    for l in range(depth):
        yp, *side = _layer_call(ctx_cfg, l, yp, mods, lw, side)
        (ys,) = _layer_call(lat_cfg, l, ys, mods, lw, (cos, sin, ck_all, cv_all, h0re, h0im))
    k_new, v_new, st_re, st_im = side

    return (yp.reshape(batch, seq, d_model), ys.reshape(dec_batch, dec_seq, d_model),
            k_new.reshape(batch, depth, seq, N_KV_HEADS, HEAD_DIM),
            v_new.reshape(batch, depth, seq, N_KV_HEADS, HEAD_DIM),
            st_re.reshape(batch, depth, 2, SSM_G, SSM_P), st_im.reshape(batch, depth, 2, SSM_G, SSM_P))
```
